```python
import jax, jax.numpy as jnp
from jax import lax
import numpy as np

D_MODEL = 1024
BATCH = 32
SEQ = 2048
DEPTH = 1

HEAD_DIM = 64
N_HEADS = D_MODEL // HEAD_DIM
N_HEADS_B = N_HEADS // 4
N_HEADS_A = N_HEADS - N_HEADS_B
WIDTH_A = N_HEADS_A * HEAD_DIM
WIDTH_B = N_HEADS_B * HEAD_DIM
MIX_WIDTH = WIDTH_A + WIDTH_B
DILATED_CONFIGS = ((128, 1), (512, 4), (2048, 16))
MOBA_BLOCK = 256
MOBA_TOPK = 3
D_FF = 2816
LN_EPS = 1e-5

kernel_name = 'hybrid_dilated_moba_macaron_deepnorm'


def layer_norm(x, g, b):
    xf = x.astype(jnp.float32)
    mu = jnp.mean(xf, axis=-1, keepdims=True)
    var = jnp.mean(jnp.square(xf - mu), axis=-1, keepdims=True)
    y = (xf - mu) * lax.rsqrt(var + LN_EPS)
    return (y * g.astype(jnp.float32) + b.astype(jnp.float32)).astype(x.dtype)


def swiglu(h, w_gate, w_up, w_down):
    return (jax.nn.silu(h @ w_gate) * (h @ w_up)) @ w_down


def alibi_slopes():
    idx = np.arange(N_HEADS)
    slopes = 2.0 ** (-8.0 * (idx + 1) / N_HEADS)
    is_b = (idx % 4) == 3
    return (jnp.asarray(slopes[~is_b], dtype=jnp.float32),
            jnp.asarray(slopes[is_b], dtype=jnp.float32))


def dilated_branch(q, k, v, slopes, window, dilation):
    H, S, Dh = q.shape
    d = dilation
    C = window // d
    s_sub = S // d
    nb = -(-s_sub // C)
    L = nb * C

    def to_sub(t):
        return t.reshape(H, s_sub, d, Dh).transpose(0, 2, 1, 3)

    qs, ks, vs = to_sub(q), to_sub(k), to_sub(v)
    qs = jnp.pad(qs, ((0, 0), (0, 0), (0, L - s_sub), (0, 0))).reshape(H, d, nb, C, Dh)

    def band(t):
        tp = jnp.pad(t, ((0, 0), (0, 0), (C, L - s_sub), (0, 0))).reshape(H, d, nb + 1, C, Dh)
        return jnp.concatenate([tp[:, :, :-1], tp[:, :, 1:]], axis=3)

    kw, vw = band(ks), band(vs)
    a = jnp.arange(C)[:, None]
    j = jnp.arange(2 * C)[None, :]
    delta = a + C - j
    key_sub = jnp.arange(nb)[:, None, None] * C - C + j
    valid = (delta >= 0) & (delta <= C) & (key_sub >= 0)

    s = jnp.einsum('hrnqd,hrnkd->hrnqk', qs, kw) * (Dh ** -0.5)
    s = s - slopes[:, None, None, None, None] * (delta * d).astype(jnp.float32)
    s = jnp.where(valid, s, -jnp.inf)
    m = jnp.max(s, axis=-1, keepdims=True)
    p = jnp.exp(s - m)
    l = jnp.sum(p, axis=-1, keepdims=True)
    o = jnp.einsum('hrnqk,hrnkd->hrnqd', p, vw) / l
    lse = (m + jnp.log(l))[..., 0]
    o = o.reshape(H, d, L, Dh)[:, :, :s_sub].transpose(0, 2, 1, 3).reshape(H, S, Dh)
    lse = lse.reshape(H, d, L)[:, :, :s_sub].transpose(0, 2, 1).reshape(H, S)
    return o, lse


def dilated_mixer(q, k, v, slopes):
    outs, lses = [], []
    for window, dilation in DILATED_CONFIGS:
        o, lse = dilated_branch(q, k, v, slopes, window, dilation)
        outs.append(o)
        lses.append(lse)
    wts = jax.nn.softmax(jnp.stack(lses), axis=0)
    return jnp.sum(wts[..., None] * jnp.stack(outs), axis=0)


def moba_mixer(q, k, v, slopes):
    H, S, Dh = q.shape
    Bk = MOBA_BLOCK
    nblk = -(-S // Bk)
    Sp = nblk * Bk
    n_sel = min(MOBA_TOPK, nblk - 1)
    scale = Dh ** -0.5

    def blocks(t):
        return jnp.pad(t, ((0, 0), (0, Sp - S), (0, 0))).reshape(H, nblk, Bk, Dh)

    qb, kb, vb = blocks(q), blocks(k), blocks(v)
    a = jnp.arange(Bk)
    causal = a[:, None] >= a[None, :]
    own_dist = (a[:, None] - a[None, :]).astype(jnp.float32)
    hidx = jnp.arange(H)[:, None, None]

    def own_scores(qn, kn):
        s = jnp.einsum('hqd,hkd->hqk', qn, kn) * scale - slopes[:, None, None] * own_dist
        return jnp.where(causal, s, -jnp.inf)

    if n_sel == 0:
        def step0(args):
            qn, kn, vn = args
            p = jax.nn.softmax(own_scores(qn, kn), axis=-1)
            return jnp.einsum('hqk,hkd->hqd', p, vn)
        o = lax.map(step0, (qb.transpose(1, 0, 2, 3), kb.transpose(1, 0, 2, 3), vb.transpose(1, 0, 2, 3)))
        return o.transpose(1, 0, 2, 3).reshape(H, Sp, Dh)[:, :S]

    kmean = jnp.mean(kb, axis=2)
    gate = jnp.einsum('hnqd,hmd->hnqm', qb, kmean)
    blk_ids = jnp.arange(nblk)
    past = blk_ids[None, :] < blk_ids[:, None]
    gate = jnp.where(past[None, :, None, :], gate, -jnp.inf)
    _, sel = lax.top_k(gate, n_sel)
    sel_ok = sel < blk_ids[None, :, None, None]

    def step(args):
        n, qn, kn, vn, seln, okn = args
        s_own = own_scores(qn, kn)
        kg = kb[hidx, seln]
        vg = vb[hidx, seln]
        dist = (n * Bk + a)[None, :, None, None] - (seln[..., None] * Bk + a[None, None, None, :])
        s_sel = jnp.einsum('hqd,hqskd->hqsk', qn, kg) * scale - slopes[:, None, None, None] * dist.astype(jnp.float32)
        s_sel = jnp.where(okn[..., None], s_sel, -jnp.inf)
        s_all = jnp.concatenate([s_own, s_sel.reshape(H, Bk, n_sel * Bk)], axis=-1)
        p = jax.nn.softmax(s_all, axis=-1)
        p_own = p[..., :Bk]
        p_sel = p[..., Bk:].reshape(H, Bk, n_sel, Bk)
        return jnp.einsum('hqk,hkd->hqd', p_own, vn) + jnp.einsum('hqsk,hqskd->hqd', p_sel, vg)

    xs = (blk_ids, qb.transpose(1, 0, 2, 3), kb.transpose(1, 0, 2, 3), vb.transpose(1, 0, 2, 3),
          sel.transpose(1, 0, 2, 3), sel_ok.transpose(1, 0, 2, 3))
    o = lax.map(step, xs)
    return o.transpose(1, 0, 2, 3).reshape(H, Sp, Dh)[:, :S]


def token_mixer(h, w_in, w_out):
    B, S, _ = h.shape
    proj = h @ w_in
    cuts = [WIDTH_A, 2 * WIDTH_A, 3 * WIDTH_A, 3 * WIDTH_A + WIDTH_B, 3 * WIDTH_A + 2 * WIDTH_B]
    qa, ka, va, qb, kb, vb = jnp.split(proj, cuts, axis=-1)

    def heads(t, nh):
        return t.reshape(B, S, nh, HEAD_DIM).transpose(0, 2, 1, 3).astype(jnp.float32)

    slopes_a, slopes_b = alibi_slopes()
    oa = lax.map(lambda t: dilated_mixer(t[0], t[1], t[2], slopes_a),
                 (heads(qa, N_HEADS_A), heads(ka, N_HEADS_A), heads(va, N_HEADS_A)))
    ob = lax.map(lambda t: moba_mixer(t[0], t[1], t[2], slopes_b),
                 (heads(qb, N_HEADS_B), heads(kb, N_HEADS_B), heads(vb, N_HEADS_B)))
    o = jnp.concatenate([oa.transpose(0, 2, 1, 3).reshape(B, S, WIDTH_A),
                         ob.transpose(0, 2, 1, 3).reshape(B, S, WIDTH_B)], axis=-1)
    return o.astype(h.dtype) @ w_out


def setup_inputs(seed: int = 0) -> dict:
    key = jax.random.key(seed)
    ks = jax.random.split(key, 16)
    beta = (8.0 * DEPTH) ** -0.25

    def nrm(k, shape, scale):
        return jax.random.normal(k, shape, jnp.float32) * scale

    col_scale = jnp.concatenate([
        jnp.ones((2 * WIDTH_A,), jnp.float32), jnp.full((WIDTH_A,), beta, jnp.float32),
        jnp.ones((2 * WIDTH_B,), jnp.float32), jnp.full((WIDTH_B,), beta, jnp.float32)])
    return {
        'x': nrm(ks[0], (BATCH, SEQ, D_MODEL), 1.0),
        'ffn1_gate': nrm(ks[1], (DEPTH, D_MODEL, D_FF), D_MODEL ** -0.5),
        'ffn1_up': nrm(ks[2], (DEPTH, D_MODEL, D_FF), D_MODEL ** -0.5),
        'ffn1_down': nrm(ks[3], (DEPTH, D_FF, D_MODEL), beta * D_FF ** -0.5),
        'ln1_g': 1.0 + nrm(ks[4], (DEPTH, D_MODEL), 0.05),
        'ln1_b': nrm(ks[5], (DEPTH, D_MODEL), 0.02),
        'w_in': nrm(ks[6], (DEPTH, D_MODEL, 3 * MIX_WIDTH), D_MODEL ** -0.5) * col_scale,
        'w_out': nrm(ks[7], (DEPTH, MIX_WIDTH, D_MODEL), beta * MIX_WIDTH ** -0.5),
        'ln2_g': 1.0 + nrm(ks[8], (DEPTH, D_MODEL), 0.05),
        'ln2_b': nrm(ks[9], (DEPTH, D_MODEL), 0.02),
        'ffn2_gate': nrm(ks[10], (DEPTH, D_MODEL, D_FF), D_MODEL ** -0.5),
        'ffn2_up': nrm(ks[11], (DEPTH, D_MODEL, D_FF), D_MODEL ** -0.5),
        'ffn2_down': nrm(ks[12], (DEPTH, D_FF, D_MODEL), beta * D_FF ** -0.5),
        'ln3_g': 1.0 + nrm(ks[13], (DEPTH, D_MODEL), 0.05),
        'ln3_b': nrm(ks[14], (DEPTH, D_MODEL), 0.02),
    }


def reference(x, ffn1_gate, ffn1_up, ffn1_down, ln1_g, ln1_b, w_in, w_out, ln2_g, ln2_b,
              ffn2_gate, ffn2_up, ffn2_down, ln3_g, ln3_b):
    alpha = (2.0 * DEPTH) ** 0.25
    h = x
    for l in range(DEPTH):
        h = layer_norm(alpha * h + 0.5 * swiglu(h, ffn1_gate[l], ffn1_up[l], ffn1_down[l]), ln1_g[l], ln1_b[l])
        h = layer_norm(alpha * h + token_mixer(h, w_in[l], w_out[l]), ln2_g[l], ln2_b[l])
        h = layer_norm(alpha * h + 0.5 * swiglu(h, ffn2_gate[l], ffn2_up[l], ffn2_down[l]), ln3_g[l], ln3_b[l])
    return h
```

```python
import functools

import numpy as np
import jax
import jax.numpy as jnp
from jax import lax
from jax.experimental import pallas as pl
from jax.experimental.pallas import tpu as pltpu

D_MODEL = 1024
SEQ = 2048
DEPTH = 1
HEAD_DIM = 64
N_HEADS = 16
N_HEADS_B = 4
N_HEADS_A = 12
WIDTH_A = N_HEADS_A * HEAD_DIM
WIDTH_B = N_HEADS_B * HEAD_DIM
MIX_WIDTH = WIDTH_A + WIDTH_B
DILATED_CONFIGS = ((128, 1), (512, 4), (2048, 16))
MOBA_BLOCK = 256
MOBA_TOPK = 3
D_FF = 2816
LN_EPS = 1e-5
ALPHA = (2.0 * DEPTH) ** 0.25

LANES = 128
N_CLASSES = 16
CLASS_ROWS = SEQ // N_CLASSES
BAND = 128
ROW_TILE = 512
CLASSES_PER_TILE = ROW_TILE // CLASS_ROWS
FF_CHUNK = 256
NEG = -1e30
VMEM_LIMIT = 56 * 1024 * 1024

F32 = jnp.float32
BF16 = jnp.bfloat16


def _dot(a, b):
    return jnp.dot(a, b, preferred_element_type=F32)


def _dot_nt(a, b, precision=None):
    return lax.dot_general(a, b, (((1,), (1,)), ((), ())), precision=precision,
                           preferred_element_type=F32)


def _layer_norm(z, g, b):
    mu = jnp.mean(z, axis=-1, keepdims=True)
    zc = z - mu
    var = jnp.mean(zc * zc, axis=-1, keepdims=True)
    return zc * lax.rsqrt(var + LN_EPS) * g + b


def _swiglu(xb, wg_ref, wu_ref, wd_ref, act_ref):
    for c in range(D_FF // FF_CHUNK):
        cols = slice(c * FF_CHUNK, (c + 1) * FF_CHUNK)
        g = _dot(xb, wg_ref[:, cols])
        u = _dot(xb, wu_ref[:, cols])
        act_ref[:, cols] = (g / (1.0 + jnp.exp(-g)) * u).astype(BF16)
    return _dot(act_ref[...], wd_ref[...])


def _ffn_qkv_kernel(x_ref, wg_ref, wu_ref, wd_ref, g_ref, b_ref, win_ref, h_ref, qkv_ref, act_ref):
    x = jnp.concatenate(
        [x_ref[0, :, c * D_MODEL:(c + 1) * D_MODEL] for c in range(CLASSES_PER_TILE)], axis=0)
    y = _swiglu(x.astype(BF16), wg_ref, wu_ref, wd_ref, act_ref)
    h = _layer_norm(ALPHA * x + 0.5 * y, g_ref[...], b_ref[...])
    h_ref[0] = h
    qkv_ref[0] = _dot(h.astype(BF16), win_ref[...])


def _const_spec(shape):
    return pl.BlockSpec(shape, lambda *_: (0,) * len(shape), pipeline_mode=pl.Buffered(1))


def _ffn_qkv(x, wg, wu, wd, g, b, w_in):
    batch = x.shape[0]
    x_view = x.reshape(batch, CLASS_ROWS, N_CLASSES * D_MODEL)
    n_tiles = SEQ // ROW_TILE
    return pl.pallas_call(
        _ffn_qkv_kernel,
        grid=(batch, n_tiles),
        in_specs=[
            pl.BlockSpec((1, CLASS_ROWS, CLASSES_PER_TILE * D_MODEL), lambda bi, ti: (bi, 0, ti)),
            _const_spec((D_MODEL, D_FF)), _const_spec((D_MODEL, D_FF)), _const_spec((D_FF, D_MODEL)),
            _const_spec((1, D_MODEL)), _const_spec((1, D_MODEL)),
            _const_spec((D_MODEL, 3 * MIX_WIDTH)),
        ],
        out_specs=[
            pl.BlockSpec((1, ROW_TILE, D_MODEL), lambda bi, ti: (bi, ti, 0)),
            pl.BlockSpec((1, ROW_TILE, 3 * MIX_WIDTH), lambda bi, ti: (bi, ti, 0)),
        ],
        out_shape=[
            jax.ShapeDtypeStruct((batch, SEQ, D_MODEL), F32),
            jax.ShapeDtypeStruct((batch, SEQ, 3 * MIX_WIDTH), F32),
        ],
        scratch_shapes=[pltpu.VMEM((ROW_TILE, D_FF), BF16)],
        compiler_params=pltpu.CompilerParams(
            dimension_semantics=("parallel", "parallel"), vmem_limit_bytes=VMEM_LIMIT),
        name="ffn1_ln1_qkv",
    )(x_view, wg, wu, wd, g, b, w_in)


def _out_ffn_kernel(oa_ref, ob_ref, h_ref, woa_ref, wob_ref, g2_ref, b2_ref,
                    wg_ref, wu_ref, wd_ref, g3_ref, b3_ref, out_ref, act_ref):
    mix = _dot(oa_ref[0], woa_ref[...]) + _dot(ob_ref[0], wob_ref[...])
    h2 = _layer_norm(ALPHA * h_ref[0] + mix, g2_ref[...], b2_ref[...])
    y = _swiglu(h2.astype(BF16), wg_ref, wu_ref, wd_ref, act_ref)
    h3 = _layer_norm(ALPHA * h2 + 0.5 * y, g3_ref[...], b3_ref[...])
    for c in range(CLASSES_PER_TILE):
        out_ref[0, :, c * D_MODEL:(c + 1) * D_MODEL] = h3[c * CLASS_ROWS:(c + 1) * CLASS_ROWS]


def _out_ffn(oa, ob, h1, woa, wob, g2, b2, wg, wu, wd, g3, b3):
    batch = h1.shape[0]
    n_tiles = SEQ // ROW_TILE
    out = pl.pallas_call(
        _out_ffn_kernel,
        grid=(batch, n_tiles),
        in_specs=[
            pl.BlockSpec((1, ROW_TILE, WIDTH_A), lambda bi, ti: (bi, ti, 0)),
            pl.BlockSpec((1, ROW_TILE, WIDTH_B), lambda bi, ti: (bi, ti, 0)),
            pl.BlockSpec((1, ROW_TILE, D_MODEL), lambda bi, ti: (bi, ti, 0)),
            _const_spec((WIDTH_A, D_MODEL)), _const_spec((WIDTH_B, D_MODEL)),
            _const_spec((1, D_MODEL)), _const_spec((1, D_MODEL)),
            _const_spec((D_MODEL, D_FF)), _const_spec((D_MODEL, D_FF)), _const_spec((D_FF, D_MODEL)),
            _const_spec((1, D_MODEL)), _const_spec((1, D_MODEL)),
        ],
        out_specs=pl.BlockSpec((1, CLASS_ROWS, CLASSES_PER_TILE * D_MODEL), lambda bi, ti: (bi, 0, ti)),
        out_shape=jax.ShapeDtypeStruct((batch, CLASS_ROWS, N_CLASSES * D_MODEL), F32),
        scratch_shapes=[pltpu.VMEM((ROW_TILE, D_FF), BF16)],
        compiler_params=pltpu.CompilerParams(
            dimension_semantics=("parallel", "parallel"), vmem_limit_bytes=VMEM_LIMIT),
        name="outproj_ln2_ffn2_ln3",
    )(oa, ob, h1, woa, wob, g2, b2, wg, wu, wd, g3, b3)
    return out.reshape(batch, SEQ, D_MODEL)


def _alibi_slopes():
    idx = np.arange(N_HEADS)
    slopes = 2.0 ** (-8.0 * (idx + 1) / N_HEADS)
    is_b = (idx % 4) == 3
    return slopes[~is_b], slopes[is_b]


def _lane_is_head0(shape):
    return lax.broadcasted_iota(jnp.int32, shape, 1) < HEAD_DIM


def _head_softmax_pv(s, vt, m_prev=None):
    m = jnp.max(s, axis=0, keepdims=True)
    if m_prev is not None:
        m = jnp.maximum(m, m_prev)
    p = jnp.exp(s - m)
    l = jnp.sum(p, axis=0, keepdims=True)
    return m, l, _dot(vt, p.astype(BF16))


def _rep_rows(row_a, row_b):
    q = row_a.shape[1]
    return jnp.concatenate([jnp.broadcast_to(row_a, (HEAD_DIM, q)),
                            jnp.broadcast_to(row_b, (HEAD_DIM, q))], axis=0)


def _dilated_pieces(branch, blk):
    if branch == 0:
        return [(CLASS_ROWS * r + 8 * blk, 8) for r in range(N_CLASSES)]
    if branch == 1:
        r4, b = divmod(blk, 4)
        return [(CLASS_ROWS * (r4 + 4 * c) + 32 * b, 32) for c in range(4)]
    return [(CLASS_ROWS * blk, CLASS_ROWS)]


def _dilated_local_pos():
    u = np.arange(BAND)
    return [16 * (u % 8) + u // 8, 4 * (u % 32) + u // 32, u]


def _dilated_bias_tables():
    slopes_a, _ = _alibi_slopes()
    pos = _dilated_local_pos()
    t12 = np.zeros((N_HEADS_A, 2, 2 * BAND, BAND), np.float32)
    t3 = np.zeros((N_HEADS_A, BAND, BAND), np.float32)
    for br in range(2):
        dil = DILATED_CONFIGS[br][1]
        a_q = pos[br][None, :]
        j_band = np.concatenate([pos[br], BAND + pos[br]])[:, None]
        delta = a_q + BAND - j_band
        valid = (delta >= 0) & (delta <= BAND)
        for h in range(N_HEADS_A):
            t12[h, br] = np.where(valid, -slopes_a[h] * delta * dil, NEG)
    dil = DILATED_CONFIGS[2][1]
    delta = pos[2][None, :] - pos[2][:, None]
    for h in range(N_HEADS_A):
        t3[h] = np.where(delta >= 0, -slopes_a[h] * delta * dil, NEG)
    return t12, t3


def _dilated_kernel(q_ref, k_ref, v_ref, t12_ref, t3_ref, o_ref,
                    qs_ref, ka_ref, kb_ref, vt_ref, acc_ref, m_ref, l_ref):
    n_blocks = SEQ // BAND
    head0 = _lane_is_head0((BAND, LANES))

    def gather(ref, pieces):
        return jnp.concatenate([ref[0, s:s + n, :] for s, n in pieces], axis=0)

    for br in range(3):
        for blk in range(n_blocks):
            pieces = _dilated_pieces(br, blk)
            rows = slice(BAND * blk, BAND * (blk + 1))
            qs_ref[br, rows, :] = (gather(q_ref, pieces) * (HEAD_DIM ** -0.5)).astype(BF16)
            kblk = gather(k_ref, pieces)
            ka_ref[br, rows, :] = jnp.where(head0, kblk, 0.0).astype(BF16)
            kb_ref[br, rows, :] = jnp.where(head0, 0.0, kblk).astype(BF16)
            vt_ref[br, :, rows] = gather(v_ref, pieces).T.astype(BF16)

    def block(br, blk, first, table):
        lo = BAND * blk if first else BAND * (blk - 1)
        keys = slice(lo, BAND * (blk + 1))
        n_keys = BAND * (blk + 1) - lo
        k2 = jnp.concatenate([ka_ref[br, keys, :], kb_ref[br, keys, :]], axis=0)
        s2 = _dot_nt(k2, qs_ref[br, BAND * blk:BAND * (blk + 1), :])
        stats = []
        for h in range(2):
            s = s2[h * n_keys:(h + 1) * n_keys] + table(h, n_keys)
            stats.append(_head_softmax_pv(s, vt_ref[br, HEAD_DIM * h:HEAD_DIM * (h + 1), keys]))
        (m0, l0, a0), (m1, l1, a1) = stats
        return (jnp.concatenate([a0, a1], axis=0).T, _rep_rows(m0, m1).T, _rep_rows(l0, l1).T)

    for br in range(2):
        for blk in range(n_blocks):
            first = (blk == 0) if br == 0 else (blk % 4 == 0)
            acc, m, l = block(br, blk, first, lambda h, nk: t12_ref[h, br, 2 * BAND - nk:, :])
            off = 0
            for s, n in _dilated_pieces(br, blk):
                acc_ref[br, s:s + n, :] = acc[off:off + n]
                m_ref[br, s:s + n, :] = m[off:off + n]
                l_ref[br, s:s + n, :] = l[off:off + n]
                off += n

    for blk in range(n_blocks):
        rows = slice(CLASS_ROWS * blk, CLASS_ROWS * (blk + 1))
        acc3, m3, l3 = block(2, blk, True, lambda h, nk: t3_ref[h])
        m1, m2 = m_ref[0, rows, :], m_ref[1, rows, :]
        m_all = jnp.maximum(jnp.maximum(m1, m2), m3)
        w1, w2, w3 = jnp.exp(m1 - m_all), jnp.exp(m2 - m_all), jnp.exp(m3 - m_all)
        num = w1 * acc_ref[0, rows, :] + w2 * acc_ref[1, rows, :] + w3 * acc3
        den = w1 * l_ref[0, rows, :] + w2 * l_ref[1, rows, :] + w3 * l3
        o_ref[0, rows, :] = (num / den).astype(BF16)


def _dilated_attention(qkv, t12, t3):
    batch = qkv.shape[0]
    n_pairs = N_HEADS_A // 2
    col = lambda base: (lambda bi, pi: (bi, 0, base + pi))
    return pl.pallas_call(
        _dilated_kernel,
        grid=(batch, n_pairs),
        in_specs=[
            pl.BlockSpec((1, SEQ, LANES), col(0)),
            pl.BlockSpec((1, SEQ, LANES), col(WIDTH_A // LANES)),
            pl.BlockSpec((1, SEQ, LANES), col(2 * WIDTH_A // LANES)),
            pl.BlockSpec((2, 2, 2 * BAND, BAND), lambda bi, pi: (pi, 0, 0, 0)),
            pl.BlockSpec((2, BAND, BAND), lambda bi, pi: (pi, 0, 0)),
        ],
        out_specs=pl.BlockSpec((1, SEQ, LANES), lambda bi, pi: (bi, 0, pi)),
        out_shape=jax.ShapeDtypeStruct((batch, SEQ, WIDTH_A), BF16),
        scratch_shapes=[
            pltpu.VMEM((3, SEQ, LANES), BF16),
            pltpu.VMEM((3, SEQ, LANES), BF16),
            pltpu.VMEM((3, SEQ, LANES), BF16),
            pltpu.VMEM((3, LANES, SEQ), BF16),
            pltpu.VMEM((2, SEQ, LANES), F32),
            pltpu.VMEM((2, SEQ, LANES), F32),
            pltpu.VMEM((2, SEQ, LANES), F32),
        ],
        compiler_params=pltpu.CompilerParams(
            dimension_semantics=("parallel", "parallel"), vmem_limit_bytes=VMEM_LIMIT),
        name="dilated_attention",
    )(qkv, qkv, qkv, t12, t3)


N_MOBA_BLOCKS = SEQ // MOBA_BLOCK
MOBA_PIECE = MOBA_BLOCK // N_CLASSES


def _moba_pieces(blk):
    return [(CLASS_ROWS * r + MOBA_PIECE * blk, MOBA_PIECE) for r in range(N_CLASSES)]


def _moba_bias_tables():
    _, slopes_b = _alibi_slopes()
    u = np.arange(MOBA_BLOCK)
    pos = N_CLASSES * (u % MOBA_PIECE) + u // MOBA_PIECE
    delta = pos[None, :] - pos[:, None]
    t = np.zeros((N_HEADS_B, 2, MOBA_BLOCK, MOBA_BLOCK), np.float32)
    for h in range(N_HEADS_B):
        t[h, 0] = -slopes_b[h] * delta
        t[h, 1] = np.where(delta >= 0, -slopes_b[h] * delta, NEG)
    return t


def _moba_kernel(slope_ref, q_ref, k_ref, v_ref, tab_ref, o_ref, qs_ref, ka_ref, kb_ref, vt_ref):
    pair = pl.program_id(1)
    head0 = _lane_is_head0((MOBA_BLOCK, LANES))
    n_sel = min(MOBA_TOPK, N_MOBA_BLOCKS - 1)

    def gather(ref, pieces):
        return jnp.concatenate([ref[0, s:s + n, :] for s, n in pieces], axis=0)

    kmean_rows = []
    for blk in range(N_MOBA_BLOCKS):
        pieces = _moba_pieces(blk)
        rows = slice(MOBA_BLOCK * blk, MOBA_BLOCK * (blk + 1))
        qs_ref[rows, :] = (gather(q_ref, pieces) * (HEAD_DIM ** -0.5)).astype(BF16)
        kblk = gather(k_ref, pieces)
        ka_ref[rows, :] = jnp.where(head0, kblk, 0.0).astype(BF16)
        kb_ref[rows, :] = jnp.where(head0, 0.0, kblk).astype(BF16)
        vt_ref[:, rows] = gather(v_ref, pieces).T.astype(BF16)
        kmean_rows.append(jnp.sum(kblk, axis=0, keepdims=True) * (1.0 / MOBA_BLOCK))
    kmean = jnp.concatenate(kmean_rows, axis=0)
    head0_k = _lane_is_head0((N_MOBA_BLOCKS, LANES))
    kmean_h = [jnp.where(head0_k, kmean, 0.0), jnp.where(head0_k, 0.0, kmean)]
    blk_id = lax.broadcasted_iota(jnp.int32, (N_MOBA_BLOCKS, MOBA_BLOCK), 0)

    for n in range(N_MOBA_BLOCKS):
        q_rows = slice(MOBA_BLOCK * n, MOBA_BLOCK * (n + 1))
        selected = [[None] * n for _ in range(2)]
        if n > n_sel:
            q_f32 = gather(q_ref, _moba_pieces(n))
            for h in range(2):
                gate = _dot_nt(kmean_h[h], q_f32, precision=lax.Precision.HIGHEST)
                for m in range(n):
                    g_m = gate[m:m + 1, :]
                    beats = ((gate > g_m) | ((gate == g_m) & (blk_id < m))) & (blk_id < n)
                    rank = jnp.sum(beats.astype(F32), axis=0, keepdims=True)
                    selected[h][m] = rank < n_sel
        state = [None, None]
        for m in range(n + 1):
            k_rows = slice(MOBA_BLOCK * m, MOBA_BLOCK * (m + 1))
            k2 = jnp.concatenate([ka_ref[k_rows, :], kb_ref[k_rows, :]], axis=0)
            s2 = _dot_nt(k2, qs_ref[q_rows, :])
            for h in range(2):
                s = s2[h * MOBA_BLOCK:(h + 1) * MOBA_BLOCK] + tab_ref[h, 1 if m == n else 0]
                if m < n:
                    row = jnp.full((1, MOBA_BLOCK), -float(MOBA_BLOCK * (n - m)), F32) * slope_ref[pair, h]
                    if selected[h][m] is not None:
                        row = jnp.where(selected[h][m], row, NEG)
                    s = s + row
                vt = vt_ref[HEAD_DIM * h:HEAD_DIM * (h + 1), k_rows]
                if state[h] is None:
                    state[h] = _head_softmax_pv(s, vt)
                else:
                    m_old, l_old, acc_old = state[h]
                    m_new, l_blk, acc_blk = _head_softmax_pv(s, vt, m_old)
                    scale = jnp.exp(m_old - m_new)
                    state[h] = (m_new, scale * l_old + l_blk, scale * acc_old + acc_blk)
        out_t = jnp.concatenate([state[h][2] / state[h][1] for h in range(2)], axis=0)
        out = out_t.T.astype(BF16)
        off = 0
        for s, cnt in _moba_pieces(n):
            o_ref[0, s:s + cnt, :] = out[off:off + cnt]
            off += cnt


def _moba_attention(qkv, slopes, table):
    batch = qkv.shape[0]
    n_pairs = N_HEADS_B // 2
    col = lambda base: (lambda bi, pi: (bi, 0, base + pi))
    base_q = 3 * WIDTH_A // LANES
    return pl.pallas_call(
        _moba_kernel,
        grid=(batch, n_pairs),
        in_specs=[
            pl.BlockSpec(memory_space=pltpu.SMEM),
            pl.BlockSpec((1, SEQ, LANES), col(base_q)),
            pl.BlockSpec((1, SEQ, LANES), col(base_q + WIDTH_B // LANES)),
            pl.BlockSpec((1, SEQ, LANES), col(base_q + 2 * WIDTH_B // LANES)),
            pl.BlockSpec((2, 2, MOBA_BLOCK, MOBA_BLOCK), lambda bi, pi: (pi, 0, 0, 0)),
        ],
        out_specs=pl.BlockSpec((1, SEQ, LANES), lambda bi, pi: (bi, 0, pi)),
        out_shape=jax.ShapeDtypeStruct((batch, SEQ, WIDTH_B), BF16),
        scratch_shapes=[
            pltpu.VMEM((SEQ, LANES), BF16),
            pltpu.VMEM((SEQ, LANES), BF16),
            pltpu.VMEM((SEQ, LANES), BF16),
            pltpu.VMEM((LANES, SEQ), BF16),
        ],
        compiler_params=pltpu.CompilerParams(
            dimension_semantics=("parallel", "parallel"), vmem_limit_bytes=VMEM_LIMIT),
        name="moba_attention",
    )(slopes, qkv, qkv, qkv, table)


def kernel(x, ffn1_gate, ffn1_up, ffn1_down, ln1_g, ln1_b, w_in, w_out, ln2_g, ln2_b,
           ffn2_gate, ffn2_up, ffn2_down, ln3_g, ln3_b):
    assert x.shape[1:] == (SEQ, D_MODEL) and ffn1_gate.shape == (DEPTH, D_MODEL, D_FF)
    bf = lambda w: w[0].astype(BF16)
    row = lambda p: p[0].reshape(1, D_MODEL).astype(F32)
    t12, t3 = _dilated_bias_tables()
    _, slopes_b = _alibi_slopes()

    h1, qkv = _ffn_qkv(x, bf(ffn1_gate), bf(ffn1_up), bf(ffn1_down), row(ln1_g), row(ln1_b), bf(w_in))
    oa = _dilated_attention(qkv, jnp.asarray(t12), jnp.asarray(t3))
    ob = _moba_attention(qkv, jnp.asarray(slopes_b.reshape(N_HEADS_B // 2, 2), F32),
                         jnp.asarray(_moba_bias_tables()))
    w_o = bf(w_out)
    return _out_ffn(oa, ob, h1, w_o[:WIDTH_A], w_o[WIDTH_A:], row(ln2_g), row(ln2_b),
                    bf(ffn2_gate), bf(ffn2_up), bf(ffn2_down), row(ln3_g), row(ln3_b))
```

```python
import functools

import numpy as np
import jax
import jax.numpy as jnp
from jax import lax
from jax.experimental import pallas as pl
from jax.experimental.pallas import tpu as pltpu

D_MODEL = 1024
SEQ = 2048
DEPTH = 1
HEAD_DIM = 64
N_HEADS = 16
N_HEADS_B = 4
N_HEADS_A = 12
WIDTH_A = N_HEADS_A * HEAD_DIM
WIDTH_B = N_HEADS_B * HEAD_DIM
MIX_WIDTH = WIDTH_A + WIDTH_B
DILATED_CONFIGS = ((128, 1), (512, 4), (2048, 16))
MOBA_BLOCK = 256
MOBA_TOPK = 3
D_FF = 2816
LN_EPS = 1e-5
ALPHA = (2.0 * DEPTH) ** 0.25

LANES = 128
N_CLASSES = 16
CLASS_ROWS = SEQ // N_CLASSES
BAND = 128
ROW_TILE = 512
TILE_CLASS_ROWS = ROW_TILE // N_CLASSES
FF_CHUNK = 256
NEG = -1e30
VMEM_LIMIT = 56 * 1024 * 1024

F32 = jnp.float32
BF16 = jnp.bfloat16


def _dot(a, b):
    return jnp.dot(a, b, preferred_element_type=F32)


def _dot_nt(a, b, precision=None):
    return lax.dot_general(a, b, (((1,), (1,)), ((), ())), precision=precision,
                           preferred_element_type=F32)


def _layer_norm(z, g, b):
    mu = jnp.mean(z, axis=-1, keepdims=True)
    zc = z - mu
    var = jnp.mean(zc * zc, axis=-1, keepdims=True)
    return zc * lax.rsqrt(var + LN_EPS) * g + b


def _swiglu(xb, wg_ref, wu_ref, wd_ref, act_ref):
    for c in range(D_FF // FF_CHUNK):
        cols = slice(c * FF_CHUNK, (c + 1) * FF_CHUNK)
        g = _dot(xb, wg_ref[:, cols])
        u = _dot(xb, wu_ref[:, cols])
        act_ref[:, cols] = (g / (1.0 + jnp.exp(-g)) * u).astype(BF16)
    return _dot(act_ref[...], wd_ref[...])


def _tile_permutation():
    u = np.arange(ROW_TILE)
    perm = np.zeros((ROW_TILE, ROW_TILE), np.float32)
    perm[u, N_CLASSES * (u % TILE_CLASS_ROWS) + u // TILE_CLASS_ROWS] = 1.0
    return perm


def _ffn_qkv_kernel(x_ref, wg_ref, wu_ref, wd_ref, g_ref, b_ref, win_ref, perm_ref,
                    h_ref, qkv_ref, act_ref):
    x = x_ref[0]
    y = _swiglu(x.astype(BF16), wg_ref, wu_ref, wd_ref, act_ref)
    h = _layer_norm(ALPHA * x + 0.5 * y, g_ref[...], b_ref[...])
    h_ref[0] = h
    h_res = _dot(perm_ref[...], h.astype(BF16)).astype(BF16)
    qkv = _dot(h_res, win_ref[...])
    for r in range(N_CLASSES):
        qkv_ref[0, r] = qkv[r * TILE_CLASS_ROWS:(r + 1) * TILE_CLASS_ROWS]


def _const_spec(shape):
    return pl.BlockSpec(shape, lambda *_: (0,) * len(shape), pipeline_mode=pl.Buffered(1))


def _ffn_qkv(x, wg, wu, wd, g, b, w_in, perm):
    batch = x.shape[0]
    n_tiles = SEQ // ROW_TILE
    h1, qkv = pl.pallas_call(
        _ffn_qkv_kernel,
        grid=(batch, n_tiles),
        in_specs=[
            pl.BlockSpec((1, ROW_TILE, D_MODEL), lambda bi, ti: (bi, ti, 0)),
            _const_spec((D_MODEL, D_FF)), _const_spec((D_MODEL, D_FF)), _const_spec((D_FF, D_MODEL)),
            _const_spec((1, D_MODEL)), _const_spec((1, D_MODEL)),
            _const_spec((D_MODEL, 3 * MIX_WIDTH)), _const_spec((ROW_TILE, ROW_TILE)),
        ],
        out_specs=[
            pl.BlockSpec((1, ROW_TILE, D_MODEL), lambda bi, ti: (bi, ti, 0)),
            pl.BlockSpec((1, N_CLASSES, TILE_CLASS_ROWS, 3 * MIX_WIDTH), lambda bi, ti: (bi, 0, ti, 0)),
        ],
        out_shape=[
            jax.ShapeDtypeStruct((batch, SEQ, D_MODEL), F32),
            jax.ShapeDtypeStruct((batch, N_CLASSES, CLASS_ROWS, 3 * MIX_WIDTH), F32),
        ],
        scratch_shapes=[pltpu.VMEM((ROW_TILE, D_FF), BF16)],
        compiler_params=pltpu.CompilerParams(
            dimension_semantics=("parallel", "parallel"), vmem_limit_bytes=VMEM_LIMIT),
        name="ffn1_ln1_qkv",
    )(x, wg, wu, wd, g, b, w_in, perm)
    return h1, qkv.reshape(batch, SEQ, 3 * MIX_WIDTH)


def _out_ffn_kernel(oa_ref, ob_ref, h_ref, unperm_ref, woa_ref, wob_ref, g2_ref, b2_ref,
                    wg_ref, wu_ref, wd_ref, g3_ref, b3_ref, out_ref, act_ref):
    def natural_rows(o_ref):
        o_res = jnp.concatenate([o_ref[0, r] for r in range(N_CLASSES)], axis=0)
        return _dot(unperm_ref[...], o_res).astype(BF16)

    mix = _dot(natural_rows(oa_ref), woa_ref[...]) + _dot(natural_rows(ob_ref), wob_ref[...])
    h2 = _layer_norm(ALPHA * h_ref[0] + mix, g2_ref[...], b2_ref[...])
    y = _swiglu(h2.astype(BF16), wg_ref, wu_ref, wd_ref, act_ref)
    out_ref[0] = _layer_norm(ALPHA * h2 + 0.5 * y, g3_ref[...], b3_ref[...])


def _out_ffn(oa, ob, h1, unperm, woa, wob, g2, b2, wg, wu, wd, g3, b3):
    batch = h1.shape[0]
    n_tiles = SEQ // ROW_TILE
    res_spec = lambda width: pl.BlockSpec((1, N_CLASSES, TILE_CLASS_ROWS, width),
                                          lambda bi, ti: (bi, 0, ti, 0))
    return pl.pallas_call(
        _out_ffn_kernel,
        grid=(batch, n_tiles),
        in_specs=[
            res_spec(WIDTH_A), res_spec(WIDTH_B),
            pl.BlockSpec((1, ROW_TILE, D_MODEL), lambda bi, ti: (bi, ti, 0)),
            _const_spec((ROW_TILE, ROW_TILE)),
            _const_spec((WIDTH_A, D_MODEL)), _const_spec((WIDTH_B, D_MODEL)),
            _const_spec((1, D_MODEL)), _const_spec((1, D_MODEL)),
            _const_spec((D_MODEL, D_FF)), _const_spec((D_MODEL, D_FF)), _const_spec((D_FF, D_MODEL)),
            _const_spec((1, D_MODEL)), _const_spec((1, D_MODEL)),
        ],
        out_specs=pl.BlockSpec((1, ROW_TILE, D_MODEL), lambda bi, ti: (bi, ti, 0)),
        out_shape=jax.ShapeDtypeStruct((batch, SEQ, D_MODEL), F32),
        scratch_shapes=[pltpu.VMEM((ROW_TILE, D_FF), BF16)],
        compiler_params=pltpu.CompilerParams(
            dimension_semantics=("parallel", "parallel"), vmem_limit_bytes=VMEM_LIMIT),
        name="outproj_ln2_ffn2_ln3",
    )(oa.reshape(batch, N_CLASSES, CLASS_ROWS, WIDTH_A), ob.reshape(batch, N_CLASSES, CLASS_ROWS, WIDTH_B),
      h1, unperm, woa, wob, g2, b2, wg, wu, wd, g3, b3)


def _alibi_slopes():
    idx = np.arange(N_HEADS)
    slopes = 2.0 ** (-8.0 * (idx + 1) / N_HEADS)
    is_b = (idx % 4) == 3
    return slopes[~is_b], slopes[is_b]


def _lane_is_head0(shape):
    return lax.broadcasted_iota(jnp.int32, shape, 1) < HEAD_DIM


def _head_softmax_pv(s, vt, m_prev=None):
    m = jnp.max(s, axis=0, keepdims=True)
    if m_prev is not None:
        m = jnp.maximum(m, m_prev)
    p = jnp.exp(s - m)
    l = jnp.sum(p, axis=0, keepdims=True)
    return m, l, _dot(vt, p.astype(BF16))


def _rep_rows(row_a, row_b):
    q = row_a.shape[1]
    return jnp.concatenate([jnp.broadcast_to(row_a, (HEAD_DIM, q)),
                            jnp.broadcast_to(row_b, (HEAD_DIM, q))], axis=0)


def _dilated_pieces(branch, blk):
    if branch == 0:
        return [(CLASS_ROWS * r + 8 * blk, 8) for r in range(N_CLASSES)]
    if branch == 1:
        r4, b = divmod(blk, 4)
        return [(CLASS_ROWS * (r4 + 4 * c) + 32 * b, 32) for c in range(4)]
    return [(CLASS_ROWS * blk, CLASS_ROWS)]


def _dilated_local_pos():
    u = np.arange(BAND)
    return [16 * (u % 8) + u // 8, 4 * (u % 32) + u // 32, u]


def _dilated_bias_tables():
    slopes_a, _ = _alibi_slopes()
    pos = _dilated_local_pos()
    t12 = np.zeros((N_HEADS_A, 2, 2 * BAND, BAND), np.float32)
    t3 = np.zeros((N_HEADS_A, BAND, BAND), np.float32)
    for br in range(2):
        dil = DILATED_CONFIGS[br][1]
        a_q = pos[br][None, :]
        j_band = np.concatenate([pos[br], BAND + pos[br]])[:, None]
        delta = a_q + BAND - j_band
        valid = (delta >= 0) & (delta <= BAND)
        for h in range(N_HEADS_A):
            t12[h, br] = np.where(valid, -slopes_a[h] * delta * dil, NEG)
    dil = DILATED_CONFIGS[2][1]
    delta = pos[2][None, :] - pos[2][:, None]
    for h in range(N_HEADS_A):
        t3[h] = np.where(delta >= 0, -slopes_a[h] * delta * dil, NEG)
    return t12, t3


def _dilated_kernel(q_ref, k_ref, v_ref, t12_ref, t3_ref, o_ref,
                    qs_ref, ka_ref, kb_ref, vt_ref, acc_ref, m_ref, l_ref):
    n_blocks = SEQ // BAND
    head0 = _lane_is_head0((BAND, LANES))

    def gather(ref, pieces):
        return jnp.concatenate([ref[0, s:s + n, :] for s, n in pieces], axis=0)

    for br in range(3):
        for blk in range(n_blocks):
            pieces = _dilated_pieces(br, blk)
            rows = slice(BAND * blk, BAND * (blk + 1))
            qs_ref[br, rows, :] = (gather(q_ref, pieces) * (HEAD_DIM ** -0.5)).astype(BF16)
            kblk = gather(k_ref, pieces)
            ka_ref[br, rows, :] = jnp.where(head0, kblk, 0.0).astype(BF16)
            kb_ref[br, rows, :] = jnp.where(head0, 0.0, kblk).astype(BF16)
            vt_ref[br, :, rows] = gather(v_ref, pieces).T.astype(BF16)

    def block(br, blk, first, table):
        lo = BAND * blk if first else BAND * (blk - 1)
        keys = slice(lo, BAND * (blk + 1))
        n_keys = BAND * (blk + 1) - lo
        k2 = jnp.concatenate([ka_ref[br, keys, :], kb_ref[br, keys, :]], axis=0)
        s2 = _dot_nt(k2, qs_ref[br, BAND * blk:BAND * (blk + 1), :])
        stats = []
        for h in range(2):
            s = s2[h * n_keys:(h + 1) * n_keys] + table(h, n_keys)
            stats.append(_head_softmax_pv(s, vt_ref[br, HEAD_DIM * h:HEAD_DIM * (h + 1), keys]))
        (m0, l0, a0), (m1, l1, a1) = stats
        return (jnp.concatenate([a0, a1], axis=0).T, _rep_rows(m0, m1).T, _rep_rows(l0, l1).T)

    for br in range(2):
        for blk in range(n_blocks):
            first = (blk == 0) if br == 0 else (blk % 4 == 0)
            acc, m, l = block(br, blk, first, lambda h, nk: t12_ref[h, br, 2 * BAND - nk:, :])
            off = 0
            for s, n in _dilated_pieces(br, blk):
                acc_ref[br, s:s + n, :] = acc[off:off + n]
                m_ref[br, s:s + n, :] = m[off:off + n]
                l_ref[br, s:s + n, :] = l[off:off + n]
                off += n

    for blk in range(n_blocks):
        rows = slice(CLASS_ROWS * blk, CLASS_ROWS * (blk + 1))
        acc3, m3, l3 = block(2, blk, True, lambda h, nk: t3_ref[h])
        m1, m2 = m_ref[0, rows, :], m_ref[1, rows, :]
        m_all = jnp.maximum(jnp.maximum(m1, m2), m3)
        w1, w2, w3 = jnp.exp(m1 - m_all), jnp.exp(m2 - m_all), jnp.exp(m3 - m_all)
        num = w1 * acc_ref[0, rows, :] + w2 * acc_ref[1, rows, :] + w3 * acc3
        den = w1 * l_ref[0, rows, :] + w2 * l_ref[1, rows, :] + w3 * l3
        o_ref[0, rows, :] = (num / den).astype(BF16)


def _dilated_attention(qkv, t12, t3):
    batch = qkv.shape[0]
    n_pairs = N_HEADS_A // 2
    col = lambda base: (lambda bi, pi: (bi, 0, base + pi))
    return pl.pallas_call(
        _dilated_kernel,
        grid=(batch, n_pairs),
        in_specs=[
            pl.BlockSpec((1, SEQ, LANES), col(0)),
            pl.BlockSpec((1, SEQ, LANES), col(WIDTH_A // LANES)),
            pl.BlockSpec((1, SEQ, LANES), col(2 * WIDTH_A // LANES)),
            pl.BlockSpec((2, 2, 2 * BAND, BAND), lambda bi, pi: (pi, 0, 0, 0)),
            pl.BlockSpec((2, BAND, BAND), lambda bi, pi: (pi, 0, 0)),
        ],
        out_specs=pl.BlockSpec((1, SEQ, LANES), lambda bi, pi: (bi, 0, pi)),
        out_shape=jax.ShapeDtypeStruct((batch, SEQ, WIDTH_A), BF16),
        scratch_shapes=[
            pltpu.VMEM((3, SEQ, LANES), BF16),
            pltpu.VMEM((3, SEQ, LANES), BF16),
            pltpu.VMEM((3, SEQ, LANES), BF16),
            pltpu.VMEM((3, LANES, SEQ), BF16),
            pltpu.VMEM((2, SEQ, LANES), F32),
            pltpu.VMEM((2, SEQ, LANES), F32),
            pltpu.VMEM((2, SEQ, LANES), F32),
        ],
        compiler_params=pltpu.CompilerParams(
            dimension_semantics=("parallel", "parallel"), vmem_limit_bytes=VMEM_LIMIT),
        name="dilated_attention",
    )(qkv, qkv, qkv, t12, t3)


N_MOBA_BLOCKS = SEQ // MOBA_BLOCK
MOBA_PIECE = MOBA_BLOCK // N_CLASSES


def _moba_pieces(blk):
    return [(CLASS_ROWS * r + MOBA_PIECE * blk, MOBA_PIECE) for r in range(N_CLASSES)]


def _moba_bias_tables():
    _, slopes_b = _alibi_slopes()
    u = np.arange(MOBA_BLOCK)
    pos = N_CLASSES * (u % MOBA_PIECE) + u // MOBA_PIECE
    delta = pos[None, :] - pos[:, None]
    t = np.zeros((N_HEADS_B, 2, MOBA_BLOCK, MOBA_BLOCK), np.float32)
    for h in range(N_HEADS_B):
        t[h, 0] = -slopes_b[h] * delta
        t[h, 1] = np.where(delta >= 0, -slopes_b[h] * delta, NEG)
    return t


def _moba_kernel(slope_ref, q_ref, k_ref, v_ref, tab_ref, o_ref, qs_ref, ka_ref, kb_ref, vt_ref):
    pair = pl.program_id(1)
    head0 = _lane_is_head0((MOBA_BLOCK, LANES))
    n_sel = min(MOBA_TOPK, N_MOBA_BLOCKS - 1)

    def gather(ref, pieces):
        return jnp.concatenate([ref[0, s:s + n, :] for s, n in pieces], axis=0)

    kmean_rows = []
    for blk in range(N_MOBA_BLOCKS):
        pieces = _moba_pieces(blk)
        rows = slice(MOBA_BLOCK * blk, MOBA_BLOCK * (blk + 1))
        qs_ref[rows, :] = (gather(q_ref, pieces) * (HEAD_DIM ** -0.5)).astype(BF16)
        kblk = gather(k_ref, pieces)
        ka_ref[rows, :] = jnp.where(head0, kblk, 0.0).astype(BF16)
        kb_ref[rows, :] = jnp.where(head0, 0.0, kblk).astype(BF16)
        vt_ref[:, rows] = gather(v_ref, pieces).T.astype(BF16)
        kmean_rows.append(jnp.sum(kblk, axis=0, keepdims=True) * (1.0 / MOBA_BLOCK))
    kmean = jnp.concatenate(kmean_rows, axis=0)
    head0_k = _lane_is_head0((N_MOBA_BLOCKS, LANES))
    kmean_h = [jnp.where(head0_k, kmean, 0.0), jnp.where(head0_k, 0.0, kmean)]
    blk_id = lax.broadcasted_iota(jnp.int32, (N_MOBA_BLOCKS, MOBA_BLOCK), 0)

    for n in range(N_MOBA_BLOCKS):
        q_rows = slice(MOBA_BLOCK * n, MOBA_BLOCK * (n + 1))
        selected = [[None] * n for _ in range(2)]
        if n > n_sel:
            q_f32 = gather(q_ref, _moba_pieces(n))
            for h in range(2):
                gate = _dot_nt(kmean_h[h], q_f32, precision=lax.Precision.HIGHEST)
                for m in range(n):
                    g_m = gate[m:m + 1, :]
                    beats = ((gate > g_m) | ((gate == g_m) & (blk_id < m))) & (blk_id < n)
                    rank = jnp.sum(beats.astype(F32), axis=0, keepdims=True)
                    selected[h][m] = rank < n_sel
        state = [None, None]
        for m in range(n + 1):
            k_rows = slice(MOBA_BLOCK * m, MOBA_BLOCK * (m + 1))
            k2 = jnp.concatenate([ka_ref[k_rows, :], kb_ref[k_rows, :]], axis=0)
            s2 = _dot_nt(k2, qs_ref[q_rows, :])
            for h in range(2):
                s = s2[h * MOBA_BLOCK:(h + 1) * MOBA_BLOCK] + tab_ref[h, 1 if m == n else 0]
                if m < n:
                    row = jnp.full((1, MOBA_BLOCK), -float(MOBA_BLOCK * (n - m)), F32) * slope_ref[pair, h]
                    if selected[h][m] is not None:
                        row = jnp.where(selected[h][m], row, NEG)
                    s = s + row
                vt = vt_ref[HEAD_DIM * h:HEAD_DIM * (h + 1), k_rows]
                if state[h] is None:
                    state[h] = _head_softmax_pv(s, vt)
                else:
                    m_old, l_old, acc_old = state[h]
                    m_new, l_blk, acc_blk = _head_softmax_pv(s, vt, m_old)
                    scale = jnp.exp(m_old - m_new)
                    state[h] = (m_new, scale * l_old + l_blk, scale * acc_old + acc_blk)
        out_t = jnp.concatenate([state[h][2] / state[h][1] for h in range(2)], axis=0)
        out = out_t.T.astype(BF16)
        off = 0
        for s, cnt in _moba_pieces(n):
            o_ref[0, s:s + cnt, :] = out[off:off + cnt]
            off += cnt


def _moba_attention(qkv, slopes, table):
    batch = qkv.shape[0]
    n_pairs = N_HEADS_B // 2
    col = lambda base: (lambda bi, pi: (bi, 0, base + pi))
    base_q = 3 * WIDTH_A // LANES
    return pl.pallas_call(
        _moba_kernel,
        grid=(batch, n_pairs),
        in_specs=[
            pl.BlockSpec(memory_space=pltpu.SMEM),
            pl.BlockSpec((1, SEQ, LANES), col(base_q)),
            pl.BlockSpec((1, SEQ, LANES), col(base_q + WIDTH_B // LANES)),
            pl.BlockSpec((1, SEQ, LANES), col(base_q + 2 * WIDTH_B // LANES)),
            pl.BlockSpec((2, 2, MOBA_BLOCK, MOBA_BLOCK), lambda bi, pi: (pi, 0, 0, 0)),
        ],
        out_specs=pl.BlockSpec((1, SEQ, LANES), lambda bi, pi: (bi, 0, pi)),
        out_shape=jax.ShapeDtypeStruct((batch, SEQ, WIDTH_B), BF16),
        scratch_shapes=[
            pltpu.VMEM((SEQ, LANES), BF16),
            pltpu.VMEM((SEQ, LANES), BF16),
            pltpu.VMEM((SEQ, LANES), BF16),
            pltpu.VMEM((LANES, SEQ), BF16),
        ],
        compiler_params=pltpu.CompilerParams(
            dimension_semantics=("parallel", "parallel"), vmem_limit_bytes=VMEM_LIMIT),
        name="moba_attention",
    )(slopes, qkv, qkv, qkv, table)


def kernel(x, ffn1_gate, ffn1_up, ffn1_down, ln1_g, ln1_b, w_in, w_out, ln2_g, ln2_b,
           ffn2_gate, ffn2_up, ffn2_down, ln3_g, ln3_b):
    assert x.shape[1:] == (SEQ, D_MODEL) and ffn1_gate.shape == (DEPTH, D_MODEL, D_FF)
    bf = lambda w: w[0].astype(BF16)
    row = lambda p: p[0].reshape(1, D_MODEL).astype(F32)
    t12, t3 = _dilated_bias_tables()
    _, slopes_b = _alibi_slopes()

    perm = _tile_permutation()
    h1, qkv = _ffn_qkv(x, bf(ffn1_gate), bf(ffn1_up), bf(ffn1_down), row(ln1_g), row(ln1_b), bf(w_in),
                       jnp.asarray(perm, BF16))
    oa = _dilated_attention(qkv, jnp.asarray(t12), jnp.asarray(t3))
    ob = _moba_attention(qkv, jnp.asarray(slopes_b.reshape(N_HEADS_B // 2, 2), F32),
                         jnp.asarray(_moba_bias_tables()))
    w_o = bf(w_out)
    return _out_ffn(oa, ob, h1, jnp.asarray(perm.T, BF16), w_o[:WIDTH_A], w_o[WIDTH_A:], row(ln2_g), row(ln2_b),
                    bf(ffn2_gate), bf(ffn2_up), bf(ffn2_down), row(ln3_g), row(ln3_b))
```

```python
import numpy as np
import jax
import jax.numpy as jnp
from jax import lax
from jax.experimental import pallas as pl
from jax.experimental.pallas import tpu as pltpu

D_MODEL = 1024
SEQ = 2048
DEPTH = 1
HEAD_DIM = 64
N_HEADS = 16
N_HEADS_B = 4
N_HEADS_A = 12
WIDTH_A = N_HEADS_A * HEAD_DIM
WIDTH_B = N_HEADS_B * HEAD_DIM
MIX_WIDTH = WIDTH_A + WIDTH_B
DILATED_CONFIGS = ((128, 1), (512, 4), (2048, 16))
MOBA_BLOCK = 256
MOBA_TOPK = 3
D_FF = 2816
LN_EPS = 1e-5
ALPHA = (2.0 * DEPTH) ** 0.25

LANES = 128
SUBLANES = 8
BF16_ROWS = 16
N_CLASSES = 16
CLASS_ROWS = SEQ // N_CLASSES
BAND = 128
ROW_TILE = 512
TILE_CLASS_ROWS = ROW_TILE // N_CLASSES
FF_CHUNK = 256
NEG = -1e30
LOG2E = 1.4426950408889634
VMEM_LIMIT = 56 * 1024 * 1024

F32 = jnp.float32
BF16 = jnp.bfloat16


def _dot(a, b):
    return jnp.dot(a, b, preferred_element_type=F32)


def _dot_nt(a, b, precision=None):
    return lax.dot_general(a, b, (((1,), (1,)), ((), ())), precision=precision,
                           preferred_element_type=F32)


def _layer_norm(z, g, b):
    mu = jnp.mean(z, axis=-1, keepdims=True)
    zc = z - mu
    var = jnp.mean(zc * zc, axis=-1, keepdims=True)
    return zc * lax.rsqrt(var + LN_EPS) * g + b


def _swiglu(xb, wg_ref, wu_ref, wd_ref, act_ref):
    for c in range(D_FF // FF_CHUNK):
        cols = slice(c * FF_CHUNK, (c + 1) * FF_CHUNK)
        g = _dot(xb, wg_ref[:, cols])
        u = _dot(xb, wu_ref[:, cols])
        act_ref[:, cols] = (g / (1.0 + jnp.exp(-g)) * u).astype(BF16)
    return _dot(act_ref[...], wd_ref[...])


def _tile_permutation():
    u = np.arange(ROW_TILE)
    perm = np.zeros((ROW_TILE, ROW_TILE), np.float32)
    perm[u, N_CLASSES * (u % TILE_CLASS_ROWS) + u // TILE_CLASS_ROWS] = 1.0
    return perm


def _ffn_qkv_kernel(x_ref, wg_ref, wu_ref, wd_ref, g_ref, b_ref, win_ref, perm_ref,
                    h_ref, qkv_ref, act_ref):
    x = x_ref[0]
    y = _swiglu(x.astype(BF16), wg_ref, wu_ref, wd_ref, act_ref)
    h = _layer_norm(ALPHA * x + 0.5 * y, g_ref[...], b_ref[...])
    h_ref[0] = h
    h_res = _dot(perm_ref[...], h.astype(BF16)).astype(BF16)
    qkv = _dot(h_res, win_ref[...])
    for r in range(N_CLASSES):
        qkv_ref[0, r] = qkv[r * TILE_CLASS_ROWS:(r + 1) * TILE_CLASS_ROWS]


def _const_spec(shape):
    return pl.BlockSpec(shape, lambda *_: (0,) * len(shape), pipeline_mode=pl.Buffered(1))


def _ffn_qkv(x, wg, wu, wd, g, b, w_in, perm):
    batch = x.shape[0]
    n_tiles = SEQ // ROW_TILE
    h1, qkv = pl.pallas_call(
        _ffn_qkv_kernel,
        grid=(batch, n_tiles),
        in_specs=[
            pl.BlockSpec((1, ROW_TILE, D_MODEL), lambda bi, ti: (bi, ti, 0)),
            _const_spec((D_MODEL, D_FF)), _const_spec((D_MODEL, D_FF)), _const_spec((D_FF, D_MODEL)),
            _const_spec((1, D_MODEL)), _const_spec((1, D_MODEL)),
            _const_spec((D_MODEL, 3 * MIX_WIDTH)), _const_spec((ROW_TILE, ROW_TILE)),
        ],
        out_specs=[
            pl.BlockSpec((1, ROW_TILE, D_MODEL), lambda bi, ti: (bi, ti, 0)),
            pl.BlockSpec((1, N_CLASSES, TILE_CLASS_ROWS, 3 * MIX_WIDTH), lambda bi, ti: (bi, 0, ti, 0)),
        ],
        out_shape=[
            jax.ShapeDtypeStruct((batch, SEQ, D_MODEL), F32),
            jax.ShapeDtypeStruct((batch, N_CLASSES, CLASS_ROWS, 3 * MIX_WIDTH), F32),
        ],
        scratch_shapes=[pltpu.VMEM((ROW_TILE, D_FF), BF16)],
        compiler_params=pltpu.CompilerParams(
            dimension_semantics=("parallel", "parallel"), vmem_limit_bytes=VMEM_LIMIT),
        name="ffn1_ln1_qkv",
    )(x, wg, wu, wd, g, b, w_in, perm)
    return h1, qkv.reshape(batch, SEQ, 3 * MIX_WIDTH)


def _out_ffn_kernel(oa_ref, ob_ref, h_ref, unperm_ref, woa_ref, wob_ref, g2_ref, b2_ref,
                    wg_ref, wu_ref, wd_ref, g3_ref, b3_ref, out_ref, act_ref):
    def natural_rows(o_ref):
        o_res = jnp.concatenate([o_ref[0, r] for r in range(N_CLASSES)], axis=0)
        return _dot(unperm_ref[...], o_res).astype(BF16)

    mix = _dot(natural_rows(oa_ref), woa_ref[...]) + _dot(natural_rows(ob_ref), wob_ref[...])
    h2 = _layer_norm(ALPHA * h_ref[0] + mix, g2_ref[...], b2_ref[...])
    y = _swiglu(h2.astype(BF16), wg_ref, wu_ref, wd_ref, act_ref)
    out_ref[0] = _layer_norm(ALPHA * h2 + 0.5 * y, g3_ref[...], b3_ref[...])


def _out_ffn(oa, ob, h1, unperm, woa, wob, g2, b2, wg, wu, wd, g3, b3):
    batch = h1.shape[0]
    n_tiles = SEQ // ROW_TILE
    res_spec = lambda width: pl.BlockSpec((1, N_CLASSES, TILE_CLASS_ROWS, width),
                                          lambda bi, ti: (bi, 0, ti, 0))
    return pl.pallas_call(
        _out_ffn_kernel,
        grid=(batch, n_tiles),
        in_specs=[
            res_spec(WIDTH_A), res_spec(WIDTH_B),
            pl.BlockSpec((1, ROW_TILE, D_MODEL), lambda bi, ti: (bi, ti, 0)),
            _const_spec((ROW_TILE, ROW_TILE)),
            _const_spec((WIDTH_A, D_MODEL)), _const_spec((WIDTH_B, D_MODEL)),
            _const_spec((1, D_MODEL)), _const_spec((1, D_MODEL)),
            _const_spec((D_MODEL, D_FF)), _const_spec((D_MODEL, D_FF)), _const_spec((D_FF, D_MODEL)),
            _const_spec((1, D_MODEL)), _const_spec((1, D_MODEL)),
        ],
        out_specs=pl.BlockSpec((1, ROW_TILE, D_MODEL), lambda bi, ti: (bi, ti, 0)),
        out_shape=jax.ShapeDtypeStruct((batch, SEQ, D_MODEL), F32),
        scratch_shapes=[pltpu.VMEM((ROW_TILE, D_FF), BF16)],
        compiler_params=pltpu.CompilerParams(
            dimension_semantics=("parallel", "parallel"), vmem_limit_bytes=VMEM_LIMIT),
        name="outproj_ln2_ffn2_ln3",
    )(oa.reshape(batch, N_CLASSES, CLASS_ROWS, WIDTH_A), ob.reshape(batch, N_CLASSES, CLASS_ROWS, WIDTH_B),
      h1, unperm, woa, wob, g2, b2, wg, wu, wd, g3, b3)


def _alibi_slopes():
    idx = np.arange(N_HEADS)
    slopes = 2.0 ** (-8.0 * (idx + 1) / N_HEADS)
    is_b = (idx % 4) == 3
    return slopes[~is_b], slopes[is_b]


def _lane_is_head0(shape):
    return lax.broadcasted_iota(jnp.int32, shape, 1) < HEAD_DIM


def _gather_rows(ref, pieces):
    return jnp.concatenate([ref[0, s:s + n, :] for s, n in pieces], axis=0)


def _dilated_pieces(branch, blk):
    if branch == 0:
        return [(CLASS_ROWS * r + 8 * blk, 8) for r in range(N_CLASSES)]
    if branch == 1:
        r4, b = divmod(blk, 4)
        return [(CLASS_ROWS * (r4 + 4 * c) + 32 * b, 32) for c in range(4)]
    return [(CLASS_ROWS * blk, CLASS_ROWS)]


def _dilated_local_pos():
    u = np.arange(BAND)
    return [16 * (u % 8) + u // 8, 4 * (u % 32) + u // 32, u]


def _dilated_bias_tables():
    slopes_a, _ = _alibi_slopes()
    pos = _dilated_local_pos()
    t12 = np.zeros((N_HEADS_A, 2, BAND, 2 * BAND), np.float64)
    t3 = np.zeros((N_HEADS_A, BAND, BAND), np.float64)
    for br in range(2):
        dil = DILATED_CONFIGS[br][1]
        a_q = pos[br][:, None]
        j_band = np.concatenate([pos[br], BAND + pos[br]])[None, :]
        delta = a_q + BAND - j_band
        valid = (delta >= 0) & (delta <= BAND)
        for h in range(N_HEADS_A):
            t12[h, br] = np.where(valid, -LOG2E * slopes_a[h] * delta * dil, NEG)
    dil = DILATED_CONFIGS[2][1]
    delta = pos[2][:, None] - pos[2][None, :]
    for h in range(N_HEADS_A):
        t3[h] = np.where(delta >= 0, -LOG2E * slopes_a[h] * delta * dil, NEG)
    t12 = t12.reshape(N_HEADS_A // 2, 2, 2, BAND, 2 * BAND).transpose(0, 2, 1, 3, 4)
    return (t12.reshape(N_HEADS_A // 2, 2, 2 * BAND, 2 * BAND).astype(np.float32),
            t3.reshape(N_HEADS_A // 2, 2 * BAND, BAND).astype(np.float32))


def _dilated_kernel(q_ref, k_ref, v_ref, t12_ref, t3_ref, o_ref,
                    q2_ref, ks_ref, vs_ref, acc_ref, m_ref, l_ref):
    n_blocks = SEQ // BAND
    head0 = _lane_is_head0((BAND, LANES))
    ones = jnp.ones((2 * BAND, LANES), BF16)

    for br in range(3):
        for blk in range(n_blocks):
            pieces = _dilated_pieces(br, blk)
            rows = slice(BAND * blk, BAND * (blk + 1))
            qblk = _gather_rows(q_ref, pieces) * (LOG2E * HEAD_DIM ** -0.5)
            q2_ref[br, 2 * BAND * blk:2 * BAND * blk + BAND, :] = jnp.where(head0, qblk, 0.0).astype(BF16)
            q2_ref[br, 2 * BAND * blk + BAND:2 * BAND * (blk + 1), :] = jnp.where(head0, 0.0, qblk).astype(BF16)
            ks_ref[br, rows, :] = _gather_rows(k_ref, pieces).astype(BF16)
            vs_ref[br, rows, :] = _gather_rows(v_ref, pieces).astype(BF16)

    def per_head(x):
        return jnp.where(head0, x[:BAND], x[BAND:])

    def block(br, blk, first, table):
        lo = BAND * blk if first else BAND * (blk - 1)
        keys = slice(lo, BAND * (blk + 1))
        n_keys = BAND * (blk + 1) - lo
        s2 = _dot_nt(q2_ref[br, 2 * BAND * blk:2 * BAND * (blk + 1), :], ks_ref[br, keys, :]) + table(n_keys)
        m = jnp.max(s2, axis=1, keepdims=True)
        p = jnp.exp2(s2 - m).astype(BF16)
        pv = _dot(p, jnp.concatenate([vs_ref[br, keys, :], ones[:n_keys]], axis=1))
        return (per_head(pv[:, :LANES]), per_head(jnp.broadcast_to(m, (2 * BAND, LANES))),
                per_head(pv[:, LANES:]))

    for br in range(2):
        for blk in range(n_blocks):
            first = (blk == 0) if br == 0 else (blk % 4 == 0)
            acc, m, l = block(br, blk, first, lambda nk: t12_ref[0, br, :, 2 * BAND - nk:])
            off = 0
            for s, n in _dilated_pieces(br, blk):
                acc_ref[br, s:s + n, :] = acc[off:off + n]
                m_ref[br, s:s + n, :] = m[off:off + n]
                l_ref[br, s:s + n, :] = l[off:off + n]
                off += n

    for blk in range(n_blocks):
        rows = slice(CLASS_ROWS * blk, CLASS_ROWS * (blk + 1))
        acc3, m3, l3 = block(2, blk, True, lambda nk: t3_ref[0])
        m1, m2 = m_ref[0, rows, :], m_ref[1, rows, :]
        m_all = jnp.maximum(jnp.maximum(m1, m2), m3)
        w1, w2, w3 = jnp.exp2(m1 - m_all), jnp.exp2(m2 - m_all), jnp.exp2(m3 - m_all)
        num = w1 * acc_ref[0, rows, :] + w2 * acc_ref[1, rows, :] + w3 * acc3
        den = w1 * l_ref[0, rows, :] + w2 * l_ref[1, rows, :] + w3 * l3
        o_ref[0, rows, :] = (num / den).astype(BF16)


def _dilated_attention(qkv, t12, t3):
    batch = qkv.shape[0]
    n_pairs = N_HEADS_A // 2
    col = lambda base: (lambda bi, pi: (bi, 0, base + pi))
    return pl.pallas_call(
        _dilated_kernel,
        grid=(batch, n_pairs),
        in_specs=[
            pl.BlockSpec((1, SEQ, LANES), col(0)),
            pl.BlockSpec((1, SEQ, LANES), col(WIDTH_A // LANES)),
            pl.BlockSpec((1, SEQ, LANES), col(2 * WIDTH_A // LANES)),
            pl.BlockSpec((1, 2, 2 * BAND, 2 * BAND), lambda bi, pi: (pi, 0, 0, 0)),
            pl.BlockSpec((1, 2 * BAND, BAND), lambda bi, pi: (pi, 0, 0)),
        ],
        out_specs=pl.BlockSpec((1, SEQ, LANES), lambda bi, pi: (bi, 0, pi)),
        out_shape=jax.ShapeDtypeStruct((batch, SEQ, WIDTH_A), BF16),
        scratch_shapes=[
            pltpu.VMEM((3, 2 * SEQ, LANES), BF16),
            pltpu.VMEM((3, SEQ, LANES), BF16),
            pltpu.VMEM((3, SEQ, LANES), BF16),
            pltpu.VMEM((2, SEQ, LANES), F32),
            pltpu.VMEM((2, SEQ, LANES), F32),
            pltpu.VMEM((2, SEQ, LANES), F32),
        ],
        compiler_params=pltpu.CompilerParams(
            dimension_semantics=("parallel", "parallel"), vmem_limit_bytes=VMEM_LIMIT),
        name="dilated_attention",
    )(qkv, qkv, qkv, t12, t3)


N_MOBA_BLOCKS = SEQ // MOBA_BLOCK
MOBA_PIECE = MOBA_BLOCK // N_CLASSES
SEL_LANES = 2 * N_MOBA_BLOCKS
POS_PIECES = 3
V_ROWS = HEAD_DIM + BF16_ROWS


def _moba_pieces(blk):
    return [(CLASS_ROWS * r + MOBA_PIECE * blk, MOBA_PIECE) for r in range(N_CLASSES)]


def _moba_tables():
    _, slopes_b = _alibi_slopes()
    u = np.arange(MOBA_BLOCK)
    pos = N_CLASSES * (u % MOBA_PIECE) + u // MOBA_PIECE
    kaug = np.zeros((N_HEADS_B, SEQ, LANES), np.float32)
    for h in range(N_HEADS_B):
        for blk in range(N_MOBA_BLOCKS):
            rows = slice(MOBA_BLOCK * blk, MOBA_BLOCK * (blk + 1))
            kaug[h, rows, N_MOBA_BLOCKS * (h % 2) + blk] = 1.0
            rest = LOG2E * slopes_b[h] * (MOBA_BLOCK * blk + pos).astype(np.float64)
            for piece in range(POS_PIECES):
                part = rest.astype(BF16).astype(np.float64)
                kaug[h, rows, SEL_LANES + piece] = part
                rest = rest - part
    causal = np.where(pos[None, :] >= pos[:, None], 0.0, NEG).astype(np.float32)
    return kaug.astype(BF16), causal


def _moba_kernel(q_ref, k_ref, v_ref, kaug_ref, causal_ref, o_ref,
                 qs_ref, ka_ref, kb_ref, vt_ref, s_ref, p_ref):
    head0 = _lane_is_head0((MOBA_BLOCK, LANES))
    n_sel = min(MOBA_TOPK, N_MOBA_BLOCKS - 1)
    ones_rows = jnp.ones((BF16_ROWS, MOBA_BLOCK), BF16)

    kmean_rows = []
    for blk in range(N_MOBA_BLOCKS):
        pieces = _moba_pieces(blk)
        rows = slice(MOBA_BLOCK * blk, MOBA_BLOCK * (blk + 1))
        qs_ref[rows, :] = (_gather_rows(q_ref, pieces) * (LOG2E * HEAD_DIM ** -0.5)).astype(BF16)
        kblk = _gather_rows(k_ref, pieces)
        ka_ref[rows, :] = jnp.where(head0, kblk, 0.0).astype(BF16)
        kb_ref[rows, :] = jnp.where(head0, 0.0, kblk).astype(BF16)
        vt = _gather_rows(v_ref, pieces).T
        for h in range(2):
            vt_ref[V_ROWS * h:V_ROWS * h + HEAD_DIM, rows] = vt[HEAD_DIM * h:HEAD_DIM * (h + 1)].astype(BF16)
            vt_ref[V_ROWS * h + HEAD_DIM:V_ROWS * (h + 1), rows] = ones_rows
        kmean_rows.append(jnp.sum(kblk, axis=0, keepdims=True) * (1.0 / MOBA_BLOCK))
    kmean = jnp.concatenate(kmean_rows, axis=0)
    head0_k = _lane_is_head0((N_MOBA_BLOCKS, LANES))
    kmean_h = [jnp.where(head0_k, kmean, 0.0), jnp.where(head0_k, 0.0, kmean)]
    blk_id = lax.broadcasted_iota(jnp.int32, (N_MOBA_BLOCKS, MOBA_BLOCK), 0)
    pos_rows = jnp.where(blk_id < POS_PIECES, 1.0, 0.0)
    pad_rows = jnp.zeros((LANES - SEL_LANES - N_MOBA_BLOCKS, MOBA_BLOCK), F32)

    for n in range(N_MOBA_BLOCKS):
        q_rows = slice(MOBA_BLOCK * n, MOBA_BLOCK * (n + 1))
        sel = [jnp.zeros((N_MOBA_BLOCKS, MOBA_BLOCK), F32)] * 2
        if n > n_sel:
            q_f32 = _gather_rows(q_ref, _moba_pieces(n))
            for h in range(2):
                gate = _dot_nt(kmean_h[h], q_f32, precision=lax.Precision.HIGHEST)
                for m in range(n):
                    g_m = gate[m:m + 1, :]
                    beats = ((gate > g_m) | ((gate == g_m) & (blk_id < m))) & (blk_id < n)
                    rank = jnp.sum(beats.astype(F32), axis=0, keepdims=True)
                    sel[h] = jnp.where((blk_id == m) & (rank >= n_sel), NEG, sel[h])
        q_extra = jnp.concatenate([sel[0], sel[1], pos_rows, pad_rows], axis=0).T.astype(BF16)
        q_aug = jnp.concatenate([qs_ref[q_rows, :], q_extra], axis=1)

        n_keys = MOBA_BLOCK * (n + 1)
        fold = [None, None]
        for m in range(n + 1):
            k_rows = slice(MOBA_BLOCK * m, MOBA_BLOCK * (m + 1))
            k_aug = jnp.concatenate(
                [jnp.concatenate([ka_ref[k_rows, :], kaug_ref[0, k_rows, :]], axis=1),
                 jnp.concatenate([kb_ref[k_rows, :], kaug_ref[1, k_rows, :]], axis=1)], axis=0)
            s2 = _dot_nt(k_aug, q_aug)
            for h in range(2):
                s = s2[h * MOBA_BLOCK:(h + 1) * MOBA_BLOCK]
                if m == n:
                    s = s + causal_ref[...]
                s_ref[h, k_rows, :] = s
                parts = [s[SUBLANES * i:SUBLANES * (i + 1)] for i in range(MOBA_BLOCK // SUBLANES)]
                if fold[h] is not None:
                    parts.append(fold[h])
                while len(parts) > 1:
                    parts = [jnp.maximum(a, b) for a, b in zip(parts[::2], parts[1::2])] + parts[len(parts) & ~1:]
                fold[h] = parts[0]
        acc = []
        for h in range(2):
            mx = jnp.max(fold[h], axis=0, keepdims=True)
            for m in range(n + 1):
                k_rows = slice(MOBA_BLOCK * m, MOBA_BLOCK * (m + 1))
                p_ref[h, k_rows, :] = jnp.exp2(s_ref[h, k_rows, :] - mx).astype(BF16)
            acc.append(_dot(vt_ref[V_ROWS * h:V_ROWS * (h + 1), :n_keys], p_ref[h, :n_keys, :]))
        out_t = jnp.concatenate(
            [acc[h][:HEAD_DIM] / acc[h][HEAD_DIM:HEAD_DIM + 1] for h in range(2)], axis=0)
        out = out_t.T.astype(BF16)
        off = 0
        for s, cnt in _moba_pieces(n):
            o_ref[0, s:s + cnt, :] = out[off:off + cnt]
            off += cnt


def _moba_attention(qkv, kaug, causal):
    batch = qkv.shape[0]
    n_pairs = N_HEADS_B // 2
    col = lambda base: (lambda bi, pi: (bi, 0, base + pi))
    base_q = 3 * WIDTH_A // LANES
    return pl.pallas_call(
        _moba_kernel,
        grid=(batch, n_pairs),
        in_specs=[
            pl.BlockSpec((1, SEQ, LANES), col(base_q)),
            pl.BlockSpec((1, SEQ, LANES), col(base_q + WIDTH_B // LANES)),
            pl.BlockSpec((1, SEQ, LANES), col(base_q + 2 * WIDTH_B // LANES)),
            pl.BlockSpec((2, SEQ, LANES), lambda bi, pi: (pi, 0, 0)),
            pl.BlockSpec((MOBA_BLOCK, MOBA_BLOCK), lambda bi, pi: (0, 0)),
        ],
        out_specs=pl.BlockSpec((1, SEQ, LANES), lambda bi, pi: (bi, 0, pi)),
        out_shape=jax.ShapeDtypeStruct((batch, SEQ, WIDTH_B), BF16),
        scratch_shapes=[
            pltpu.VMEM((SEQ, LANES), BF16),
            pltpu.VMEM((SEQ, LANES), BF16),
            pltpu.VMEM((SEQ, LANES), BF16),
            pltpu.VMEM((2 * V_ROWS, SEQ), BF16),
            pltpu.VMEM((2, SEQ, MOBA_BLOCK), F32),
            pltpu.VMEM((2, SEQ, MOBA_BLOCK), BF16),
        ],
        compiler_params=pltpu.CompilerParams(
            dimension_semantics=("parallel", "parallel"), vmem_limit_bytes=VMEM_LIMIT),
        name="moba_attention",
    )(qkv, qkv, qkv, kaug, causal)


def kernel(x, ffn1_gate, ffn1_up, ffn1_down, ln1_g, ln1_b, w_in, w_out, ln2_g, ln2_b,
           ffn2_gate, ffn2_up, ffn2_down, ln3_g, ln3_b):
    assert x.shape[1:] == (SEQ, D_MODEL) and ffn1_gate.shape == (DEPTH, D_MODEL, D_FF)
    bf = lambda w: w[0].astype(BF16)
    row = lambda p: p[0].reshape(1, D_MODEL).astype(F32)
    t12, t3 = _dilated_bias_tables()
    kaug, causal = _moba_tables()
    perm = _tile_permutation()

    h1, qkv = _ffn_qkv(x, bf(ffn1_gate), bf(ffn1_up), bf(ffn1_down), row(ln1_g), row(ln1_b), bf(w_in),
                       jnp.asarray(perm, BF16))
    oa = _dilated_attention(qkv, jnp.asarray(t12), jnp.asarray(t3))
    ob = _moba_attention(qkv, jnp.asarray(kaug), jnp.asarray(causal))
    w_o = bf(w_out)
    return _out_ffn(oa, ob, h1, jnp.asarray(perm.T, BF16), w_o[:WIDTH_A], w_o[WIDTH_A:], row(ln2_g), row(ln2_b),
                    bf(ffn2_gate), bf(ffn2_up), bf(ffn2_down), row(ln3_g), row(ln3_b))
```

```python
import numpy as np
import jax
import jax.numpy as jnp
from jax import lax
from jax.experimental import pallas as pl
from jax.experimental.pallas import tpu as pltpu

D_MODEL = 1024
SEQ = 2048
DEPTH = 1
HEAD_DIM = 64
N_HEADS = 16
N_HEADS_B = 4
N_HEADS_A = 12
WIDTH_A = N_HEADS_A * HEAD_DIM
WIDTH_B = N_HEADS_B * HEAD_DIM
MIX_WIDTH = WIDTH_A + WIDTH_B
DILATED_CONFIGS = ((128, 1), (512, 4), (2048, 16))
MOBA_BLOCK = 256
MOBA_TOPK = 3
D_FF = 2816
LN_EPS = 1e-5
ALPHA = (2.0 * DEPTH) ** 0.25

LANES = 128
SUBLANES = 8
BF16_ROWS = 16
N_CLASSES = 16
CLASS_ROWS = SEQ // N_CLASSES
BAND = 128
ROW_TILE = 512
TILE_CLASS_ROWS = ROW_TILE // N_CLASSES
SUB_TILE = 256
SUB_CLASS_ROWS = SUB_TILE // N_CLASSES
FF_CHUNK = 256
NEG = -1e30
LOG2E = 1.4426950408889634
VMEM_LIMIT = 56 * 1024 * 1024

F32 = jnp.float32
BF16 = jnp.bfloat16


def _dot(a, b):
    return jnp.dot(a, b, preferred_element_type=F32)


def _dot_nt(a, b, precision=None):
    return lax.dot_general(a, b, (((1,), (1,)), ((), ())), precision=precision,
                           preferred_element_type=F32)


def _layer_norm(z, g, b):
    mu = jnp.mean(z, axis=-1, keepdims=True)
    zc = z - mu
    var = jnp.mean(zc * zc, axis=-1, keepdims=True)
    return zc * lax.rsqrt(var + LN_EPS) * g + b


def _swiglu_steps(get_xb, wg_ref, wu_ref, act_ref, rows):
    def chunk(c):
        cols = slice(c * FF_CHUNK, (c + 1) * FF_CHUNK)
        xb = get_xb()
        g = _dot(xb, wg_ref[:, cols])
        u = _dot(xb, wu_ref[:, cols])
        act_ref[rows, cols] = (g / (1.0 + jnp.exp(-g)) * u).astype(BF16)

    return [lambda c=c: chunk(c) for c in range(D_FF // FF_CHUNK)]


def _emit_spread(main_steps, side_steps):
    gap = max(1, len(main_steps) // (len(side_steps) + 1))
    side = list(side_steps)
    for i, step in enumerate(main_steps):
        step()
        if side and (i + 1) % gap == 0:
            side.pop(0)()
    for step in side:
        step()


def _tile_permutation():
    u = np.arange(SUB_TILE)
    perm = np.zeros((SUB_TILE, SUB_TILE), np.float32)
    perm[u, N_CLASSES * (u % SUB_CLASS_ROWS) + u // SUB_CLASS_ROWS] = 1.0
    return perm


def _ffn_qkv_kernel(x_ref, wg_ref, wu_ref, wd_ref, g_ref, b_ref, win_ref, perm_ref,
                    h_ref, qkv_ref, act_ref):
    def sub_tile(sub):
        rows = slice(sub * SUB_TILE, (sub + 1) * SUB_TILE)
        val = {}

        def norm():
            x = x_ref[0, rows, :]
            y = _dot(act_ref[rows, :], wd_ref[...])
            h = _layer_norm(ALPHA * x + 0.5 * y, g_ref[...], b_ref[...])
            h_ref[0, rows, :] = h
            val["hb"] = h.astype(BF16)

        def reorder():
            val["hb"] = _dot(perm_ref[...], val["hb"]).astype(BF16)

        def project():
            qkv = _dot(val.pop("hb"), win_ref[...])
            for r in range(N_CLASSES):
                qkv_ref[0, r, sub * SUB_CLASS_ROWS:(sub + 1) * SUB_CLASS_ROWS, :] = (
                    qkv[r * SUB_CLASS_ROWS:(r + 1) * SUB_CLASS_ROWS])

        chunks = _swiglu_steps(lambda: x_ref[0, rows, :].astype(BF16), wg_ref, wu_ref, act_ref, rows)
        return chunks, [norm, reorder, project]

    tail = []
    for sub in range(ROW_TILE // SUB_TILE):
        chunks, next_tail = sub_tile(sub)
        _emit_spread(chunks, tail)
        tail = next_tail
    _emit_spread([], tail)


def _const_spec(shape):
    return pl.BlockSpec(shape, lambda *_: (0,) * len(shape), pipeline_mode=pl.Buffered(1))


def _ffn_qkv(x, wg, wu, wd, g, b, w_in, perm):
    batch = x.shape[0]
    n_tiles = SEQ // ROW_TILE
    h1, qkv = pl.pallas_call(
        _ffn_qkv_kernel,
        grid=(batch, n_tiles),
        in_specs=[
            pl.BlockSpec((1, ROW_TILE, D_MODEL), lambda bi, ti: (bi, ti, 0)),
            _const_spec((D_MODEL, D_FF)), _const_spec((D_MODEL, D_FF)), _const_spec((D_FF, D_MODEL)),
            _const_spec((1, D_MODEL)), _const_spec((1, D_MODEL)),
            _const_spec((D_MODEL, 3 * MIX_WIDTH)), _const_spec((SUB_TILE, SUB_TILE)),
        ],
        out_specs=[
            pl.BlockSpec((1, ROW_TILE, D_MODEL), lambda bi, ti: (bi, ti, 0)),
            pl.BlockSpec((1, N_CLASSES, TILE_CLASS_ROWS, 3 * MIX_WIDTH), lambda bi, ti: (bi, 0, ti, 0)),
        ],
        out_shape=[
            jax.ShapeDtypeStruct((batch, SEQ, D_MODEL), F32),
            jax.ShapeDtypeStruct((batch, N_CLASSES, CLASS_ROWS, 3 * MIX_WIDTH), F32),
        ],
        scratch_shapes=[pltpu.VMEM((ROW_TILE, D_FF), BF16)],
        compiler_params=pltpu.CompilerParams(
            dimension_semantics=("parallel", "parallel"), vmem_limit_bytes=VMEM_LIMIT),
        name="ffn1_ln1_qkv",
    )(x, wg, wu, wd, g, b, w_in, perm)
    return h1, qkv.reshape(batch, SEQ, 3 * MIX_WIDTH)


def _out_ffn_kernel(oa_ref, ob_ref, h_ref, unperm_ref, woa_ref, wob_ref, g2_ref, b2_ref,
                    wg_ref, wu_ref, wd_ref, g3_ref, b3_ref, out_ref, act_ref):
    def natural_rows(o_ref, sub):
        o_res = jnp.concatenate(
            [o_ref[0, r, sub * SUB_CLASS_ROWS:(sub + 1) * SUB_CLASS_ROWS, :] for r in range(N_CLASSES)], axis=0)
        return _dot(unperm_ref[...], o_res).astype(BF16)

    def sub_tile(sub):
        rows = slice(sub * SUB_TILE, (sub + 1) * SUB_TILE)
        val = {}

        def mix_norm():
            mix = (_dot(natural_rows(oa_ref, sub), woa_ref[...]) + _dot(natural_rows(ob_ref, sub), wob_ref[...]))
            val["h2"] = _layer_norm(ALPHA * h_ref[0, rows, :] + mix, g2_ref[...], b2_ref[...])
            val["h2b"] = val["h2"].astype(BF16)

        def ffn_norm():
            y = _dot(act_ref[rows, :], wd_ref[...])
            out_ref[0, rows, :] = _layer_norm(ALPHA * val.pop("h2") + 0.5 * y, g3_ref[...], b3_ref[...])

        chunks = _swiglu_steps(lambda: val["h2b"], wg_ref, wu_ref, act_ref, rows)
        return mix_norm, chunks, ffn_norm

    head0, chunks0, tail0 = sub_tile(0)
    head1, chunks1, tail1 = sub_tile(1)
    head0()
    _emit_spread(chunks0, [head1])
    _emit_spread(chunks1, [tail0])
    tail1()


def _out_ffn(oa, ob, h1, unperm, woa, wob, g2, b2, wg, wu, wd, g3, b3):
    batch = h1.shape[0]
    n_tiles = SEQ // ROW_TILE
    res_spec = lambda width: pl.BlockSpec((1, N_CLASSES, TILE_CLASS_ROWS, width),
                                          lambda bi, ti: (bi, 0, ti, 0))
    return pl.pallas_call(
        _out_ffn_kernel,
        grid=(batch, n_tiles),
        in_specs=[
            res_spec(WIDTH_A), res_spec(WIDTH_B),
            pl.BlockSpec((1, ROW_TILE, D_MODEL), lambda bi, ti: (bi, ti, 0)),
            _const_spec((SUB_TILE, SUB_TILE)),
            _const_spec((WIDTH_A, D_MODEL)), _const_spec((WIDTH_B, D_MODEL)),
            _const_spec((1, D_MODEL)), _const_spec((1, D_MODEL)),
            _const_spec((D_MODEL, D_FF)), _const_spec((D_MODEL, D_FF)), _const_spec((D_FF, D_MODEL)),
            _const_spec((1, D_MODEL)), _const_spec((1, D_MODEL)),
        ],
        out_specs=pl.BlockSpec((1, ROW_TILE, D_MODEL), lambda bi, ti: (bi, ti, 0)),
        out_shape=jax.ShapeDtypeStruct((batch, SEQ, D_MODEL), F32),
        scratch_shapes=[pltpu.VMEM((ROW_TILE, D_FF), BF16)],
        compiler_params=pltpu.CompilerParams(
            dimension_semantics=("parallel", "parallel"), vmem_limit_bytes=VMEM_LIMIT),
        name="outproj_ln2_ffn2_ln3",
    )(oa.reshape(batch, N_CLASSES, CLASS_ROWS, WIDTH_A), ob.reshape(batch, N_CLASSES, CLASS_ROWS, WIDTH_B),
      h1, unperm, woa, wob, g2, b2, wg, wu, wd, g3, b3)


def _alibi_slopes():
    idx = np.arange(N_HEADS)
    slopes = 2.0 ** (-8.0 * (idx + 1) / N_HEADS)
    is_b = (idx % 4) == 3
    return slopes[~is_b], slopes[is_b]


def _lane_is_head0(shape):
    return lax.broadcasted_iota(jnp.int32, shape, 1) < HEAD_DIM


def _gather_rows(ref, pieces):
    return jnp.concatenate([ref[0, s:s + n, :] for s, n in pieces], axis=0)


def _dilated_pieces(branch, blk):
    if branch == 0:
        return [(CLASS_ROWS * r + 8 * blk, 8) for r in range(N_CLASSES)]
    if branch == 1:
        r4, b = divmod(blk, 4)
        return [(CLASS_ROWS * (r4 + 4 * c) + 32 * b, 32) for c in range(4)]
    return [(CLASS_ROWS * blk, CLASS_ROWS)]


def _dilated_local_pos():
    u = np.arange(BAND)
    return [16 * (u % 8) + u // 8, 4 * (u % 32) + u // 32, u]


def _dilated_bias_tables():
    slopes_a, _ = _alibi_slopes()
    pos = _dilated_local_pos()
    t12 = np.zeros((N_HEADS_A, 2, BAND, 2 * BAND), np.float64)
    t3 = np.zeros((N_HEADS_A, BAND, BAND), np.float64)
    for br in range(2):
        dil = DILATED_CONFIGS[br][1]
        a_q = pos[br][:, None]
        j_band = np.concatenate([pos[br], BAND + pos[br]])[None, :]
        delta = a_q + BAND - j_band
        valid = (delta >= 0) & (delta <= BAND)
        for h in range(N_HEADS_A):
            t12[h, br] = np.where(valid, -LOG2E * slopes_a[h] * delta * dil, NEG)
    dil = DILATED_CONFIGS[2][1]
    delta = pos[2][:, None] - pos[2][None, :]
    for h in range(N_HEADS_A):
        t3[h] = np.where(delta >= 0, -LOG2E * slopes_a[h] * delta * dil, NEG)
    t12 = t12.reshape(N_HEADS_A // 2, 2, 2, BAND, 2 * BAND).transpose(0, 2, 1, 3, 4)
    return (t12.reshape(N_HEADS_A // 2, 2, 2 * BAND, 2 * BAND).astype(np.float32),
            t3.reshape(N_HEADS_A // 2, 2 * BAND, BAND).astype(np.float32))


def _dilated_kernel(q_ref, k_ref, v_ref, t12_ref, t3_ref, o_ref,
                    q2_ref, ks_ref, vs_ref, acc_ref, m_ref, l_ref):
    n_blocks = SEQ // BAND
    head0 = _lane_is_head0((BAND, LANES))
    ones = jnp.ones((2 * BAND, LANES), BF16)

    def stage(br, blk):
        pieces = _dilated_pieces(br, blk)
        rows = slice(BAND * blk, BAND * (blk + 1))
        qblk = _gather_rows(q_ref, pieces) * (LOG2E * HEAD_DIM ** -0.5)
        q2_ref[br, 2 * BAND * blk:2 * BAND * blk + BAND, :] = jnp.where(head0, qblk, 0.0).astype(BF16)
        q2_ref[br, 2 * BAND * blk + BAND:2 * BAND * (blk + 1), :] = jnp.where(head0, 0.0, qblk).astype(BF16)
        ks_ref[br, rows, :] = _gather_rows(k_ref, pieces).astype(BF16)
        vs_ref[br, rows, :] = _gather_rows(v_ref, pieces).astype(BF16)

    def per_head(x):
        return jnp.where(head0, x[:BAND], x[BAND:])

    def block_steps(br, blk):
        first = (blk == 0) if br == 0 else (blk % 4 == 0) if br == 1 else True
        lo = BAND * blk if first else BAND * (blk - 1)
        keys = slice(lo, BAND * (blk + 1))
        n_keys = BAND * (blk + 1) - lo
        val = {}

        def scores():
            table = t3_ref[0] if br == 2 else t12_ref[0, br, :, 2 * BAND - n_keys:]
            s2 = _dot_nt(q2_ref[br, 2 * BAND * blk:2 * BAND * (blk + 1), :], ks_ref[br, keys, :]) + table
            val["s"], val["m"] = s2, jnp.max(s2, axis=1, keepdims=True)

        def weigh():
            p = jnp.exp2(val.pop("s") - val["m"]).astype(BF16)
            val["pv"] = _dot(p, jnp.concatenate([vs_ref[br, keys, :], ones[:n_keys]], axis=1))

        def results():
            pv = val.pop("pv")
            acc, l = per_head(pv[:, :LANES]), per_head(pv[:, LANES:])
            m = per_head(jnp.broadcast_to(val.pop("m"), (2 * BAND, LANES)))
            if br < 2:
                off = 0
                for s, n in _dilated_pieces(br, blk):
                    acc_ref[br, s:s + n, :] = acc[off:off + n]
                    m_ref[br, s:s + n, :] = m[off:off + n]
                    l_ref[br, s:s + n, :] = l[off:off + n]
                    off += n
                return
            rows = slice(CLASS_ROWS * blk, CLASS_ROWS * (blk + 1))
            m1, m2 = m_ref[0, rows, :], m_ref[1, rows, :]
            m_all = jnp.maximum(jnp.maximum(m1, m2), m)
            w1, w2, w3 = jnp.exp2(m1 - m_all), jnp.exp2(m2 - m_all), jnp.exp2(m - m_all)
            num = w1 * acc_ref[0, rows, :] + w2 * acc_ref[1, rows, :] + w3 * acc
            den = w1 * l_ref[0, rows, :] + w2 * l_ref[1, rows, :] + w3 * l
            o_ref[0, rows, :] = (num / den).astype(BF16)

        return scores, weigh, results

    for blk in range(n_blocks):
        stage(0, blk)
    blocks = [block_steps(br, blk) for br in range(3) for blk in range(n_blocks)]
    depth = 3
    for i in range(len(blocks) + depth - 1):
        if i < 2 * n_blocks:
            stage(1 + i // n_blocks, i % n_blocks)
        for d in range(depth):
            if 0 <= i - d < len(blocks):
                blocks[i - d][d]()


def _dilated_attention(qkv, t12, t3):
    batch = qkv.shape[0]
    n_pairs = N_HEADS_A // 2
    col = lambda base: (lambda bi, pi: (bi, 0, base + pi))
    return pl.pallas_call(
        _dilated_kernel,
        grid=(batch, n_pairs),
        in_specs=[
            pl.BlockSpec((1, SEQ, LANES), col(0)),
            pl.BlockSpec((1, SEQ, LANES), col(WIDTH_A // LANES)),
            pl.BlockSpec((1, SEQ, LANES), col(2 * WIDTH_A // LANES)),
            pl.BlockSpec((1, 2, 2 * BAND, 2 * BAND), lambda bi, pi: (pi, 0, 0, 0)),
            pl.BlockSpec((1, 2 * BAND, BAND), lambda bi, pi: (pi, 0, 0)),
        ],
        out_specs=pl.BlockSpec((1, SEQ, LANES), lambda bi, pi: (bi, 0, pi)),
        out_shape=jax.ShapeDtypeStruct((batch, SEQ, WIDTH_A), BF16),
        scratch_shapes=[
            pltpu.VMEM((3, 2 * SEQ, LANES), BF16),
            pltpu.VMEM((3, SEQ, LANES), BF16),
            pltpu.VMEM((3, SEQ, LANES), BF16),
            pltpu.VMEM((2, SEQ, LANES), F32),
            pltpu.VMEM((2, SEQ, LANES), F32),
            pltpu.VMEM((2, SEQ, LANES), F32),
        ],
        compiler_params=pltpu.CompilerParams(
            dimension_semantics=("parallel", "parallel"), vmem_limit_bytes=VMEM_LIMIT),
        name="dilated_attention",
    )(qkv, qkv, qkv, t12, t3)


N_MOBA_BLOCKS = SEQ // MOBA_BLOCK
MOBA_PIECE = MOBA_BLOCK // N_CLASSES
SEL_LANES = 2 * N_MOBA_BLOCKS
POS_PIECES = 3
V_ROWS = HEAD_DIM + BF16_ROWS


def _moba_pieces(blk):
    return [(CLASS_ROWS * r + MOBA_PIECE * blk, MOBA_PIECE) for r in range(N_CLASSES)]


def _moba_tables():
    _, slopes_b = _alibi_slopes()
    u = np.arange(MOBA_BLOCK)
    pos = N_CLASSES * (u % MOBA_PIECE) + u // MOBA_PIECE
    kaug = np.zeros((N_HEADS_B, SEQ, LANES), np.float32)
    for h in range(N_HEADS_B):
        for blk in range(N_MOBA_BLOCKS):
            rows = slice(MOBA_BLOCK * blk, MOBA_BLOCK * (blk + 1))
            kaug[h, rows, N_MOBA_BLOCKS * (h % 2) + blk] = 1.0
            rest = LOG2E * slopes_b[h] * (MOBA_BLOCK * blk + pos).astype(np.float64)
            for piece in range(POS_PIECES):
                part = rest.astype(BF16).astype(np.float64)
                kaug[h, rows, SEL_LANES + piece] = part
                rest = rest - part
    causal = np.where(pos[None, :] >= pos[:, None], 0.0, NEG).astype(np.float32)
    return kaug.astype(BF16), causal


def _moba_kernel(q_ref, k_ref, v_ref, kaug_ref, causal_ref, o_ref,
                 qs_ref, ka_ref, kb_ref, vt_ref, s_ref, p_ref):
    head0 = _lane_is_head0((MOBA_BLOCK, LANES))
    n_sel = min(MOBA_TOPK, N_MOBA_BLOCKS - 1)
    ones_rows = jnp.ones((BF16_ROWS, MOBA_BLOCK), BF16)

    kmean_rows = []
    for blk in range(N_MOBA_BLOCKS):
        pieces = _moba_pieces(blk)
        rows = slice(MOBA_BLOCK * blk, MOBA_BLOCK * (blk + 1))
        qs_ref[rows, :LANES] = (_gather_rows(q_ref, pieces) * (LOG2E * HEAD_DIM ** -0.5)).astype(BF16)
        kblk = _gather_rows(k_ref, pieces)
        ka_ref[rows, :] = jnp.where(head0, kblk, 0.0).astype(BF16)
        kb_ref[rows, :] = jnp.where(head0, 0.0, kblk).astype(BF16)
        vt = _gather_rows(v_ref, pieces).T
        for h in range(2):
            vt_ref[V_ROWS * h:V_ROWS * h + HEAD_DIM, rows] = vt[HEAD_DIM * h:HEAD_DIM * (h + 1)].astype(BF16)
            vt_ref[V_ROWS * h + HEAD_DIM:V_ROWS * (h + 1), rows] = ones_rows
        kmean_rows.append(jnp.sum(kblk, axis=0, keepdims=True) * (1.0 / MOBA_BLOCK))
    kmean = jnp.concatenate(kmean_rows, axis=0)
    head0_k = _lane_is_head0((N_MOBA_BLOCKS, LANES))
    kmean_h = [jnp.where(head0_k, kmean, 0.0), jnp.where(head0_k, 0.0, kmean)]
    blk_id = lax.broadcasted_iota(jnp.int32, (N_MOBA_BLOCKS, MOBA_BLOCK), 0)
    pos_rows = jnp.where(blk_id < POS_PIECES, 1.0, 0.0)
    pad_rows = jnp.zeros((LANES - SEL_LANES - N_MOBA_BLOCKS, MOBA_BLOCK), F32)

    def query_extra_lanes(n):
        sel = [jnp.zeros((N_MOBA_BLOCKS, MOBA_BLOCK), F32)] * 2
        if n > n_sel:
            q_f32 = _gather_rows(q_ref, _moba_pieces(n))
            for h in range(2):
                gate = _dot_nt(kmean_h[h], q_f32, precision=lax.Precision.HIGHEST)
                for m in range(n):
                    g_m = gate[m:m + 1, :]
                    beats = ((gate > g_m) | ((gate == g_m) & (blk_id < m))) & (blk_id < n)
                    rank = jnp.sum(beats.astype(F32), axis=0, keepdims=True)
                    sel[h] = jnp.where((blk_id == m) & (rank >= n_sel), NEG, sel[h])
        q_extra = jnp.concatenate([sel[0], sel[1], pos_rows, pad_rows], axis=0).T
        qs_ref[MOBA_BLOCK * n:MOBA_BLOCK * (n + 1), LANES:] = q_extra.astype(BF16)

    def score_steps(n, slot, fold):
        def step(m):
            k_rows = slice(MOBA_BLOCK * m, MOBA_BLOCK * (m + 1))
            k_aug = jnp.concatenate(
                [jnp.concatenate([ka_ref[k_rows, :], kaug_ref[0, k_rows, :]], axis=1),
                 jnp.concatenate([kb_ref[k_rows, :], kaug_ref[1, k_rows, :]], axis=1)], axis=0)
            s2 = _dot_nt(k_aug, qs_ref[MOBA_BLOCK * n:MOBA_BLOCK * (n + 1), :])
            for h in range(2):
                s = s2[h * MOBA_BLOCK:(h + 1) * MOBA_BLOCK]
                if m == n:
                    s = s + causal_ref[...]
                s_ref[slot, h, k_rows, :] = s
                parts = [s[SUBLANES * i:SUBLANES * (i + 1)] for i in range(MOBA_BLOCK // SUBLANES)]
                if fold[h] is not None:
                    parts.append(fold[h])
                while len(parts) > 1:
                    parts = [jnp.maximum(a, b) for a, b in zip(parts[::2], parts[1::2])] + parts[len(parts) & ~1:]
                fold[h] = parts[0]

        return [lambda m=m: step(m) for m in range(n + 1)]

    def output_steps(n, slot, fold):
        n_keys = MOBA_BLOCK * (n + 1)
        mx = []

        def probs(m):
            if not mx:
                mx.extend(jnp.max(fold[h], axis=0, keepdims=True) for h in range(2))
            k_rows = slice(MOBA_BLOCK * m, MOBA_BLOCK * (m + 1))
            for h in range(2):
                p_ref[slot, h, k_rows, :] = jnp.exp2(s_ref[slot, h, k_rows, :] - mx[h]).astype(BF16)

        def finish():
            acc = [_dot(vt_ref[V_ROWS * h:V_ROWS * (h + 1), :n_keys], p_ref[slot, h, :n_keys, :])
                   for h in range(2)]
            out_t = jnp.concatenate(
                [acc[h][:HEAD_DIM] / acc[h][HEAD_DIM:HEAD_DIM + 1] for h in range(2)], axis=0)
            out = out_t.T.astype(BF16)
            off = 0
            for s, cnt in _moba_pieces(n):
                o_ref[0, s:s + cnt, :] = out[off:off + cnt]
                off += cnt

        return [lambda m=m: probs(m) for m in range(n + 1)] + [finish]

    order = list(range(N_MOBA_BLOCKS))
    late = [n for n in order if n > n_sel]
    for n in order:
        if n <= n_sel:
            query_extra_lanes(n)
    folds = [[None, None] for _ in range(N_MOBA_BLOCKS)]
    for pos in range(N_MOBA_BLOCKS + 1):
        first = score_steps(order[pos], pos % 2, folds[order[pos]]) if pos < N_MOBA_BLOCKS else []
        second = output_steps(order[pos - 1], (pos - 1) % 2, folds[order[pos - 1]]) if pos > 0 else []
        third = [lambda n=n: query_extra_lanes(n) for n in late[2 * pos:2 * pos + 2]]
        for i in range(max(len(first), len(second), len(third))):
            for steps in (first, second, third):
                if i < len(steps):
                    steps[i]()


def _moba_attention(qkv, kaug, causal):
    batch = qkv.shape[0]
    n_pairs = N_HEADS_B // 2
    col = lambda base: (lambda bi, pi: (bi, 0, base + pi))
    base_q = 3 * WIDTH_A // LANES
    return pl.pallas_call(
        _moba_kernel,
        grid=(batch, n_pairs),
        in_specs=[
            pl.BlockSpec((1, SEQ, LANES), col(base_q)),
            pl.BlockSpec((1, SEQ, LANES), col(base_q + WIDTH_B // LANES)),
            pl.BlockSpec((1, SEQ, LANES), col(base_q + 2 * WIDTH_B // LANES)),
            pl.BlockSpec((2, SEQ, LANES), lambda bi, pi: (pi, 0, 0)),
            pl.BlockSpec((MOBA_BLOCK, MOBA_BLOCK), lambda bi, pi: (0, 0)),
        ],
        out_specs=pl.BlockSpec((1, SEQ, LANES), lambda bi, pi: (bi, 0, pi)),
        out_shape=jax.ShapeDtypeStruct((batch, SEQ, WIDTH_B), BF16),
        scratch_shapes=[
            pltpu.VMEM((SEQ, 2 * LANES), BF16),
            pltpu.VMEM((SEQ, LANES), BF16),
            pltpu.VMEM((SEQ, LANES), BF16),
            pltpu.VMEM((2 * V_ROWS, SEQ), BF16),
            pltpu.VMEM((2, 2, SEQ, MOBA_BLOCK), F32),
            pltpu.VMEM((2, 2, SEQ, MOBA_BLOCK), BF16),
        ],
        compiler_params=pltpu.CompilerParams(
            dimension_semantics=("parallel", "parallel"), vmem_limit_bytes=VMEM_LIMIT),
        name="moba_attention",
    )(qkv, qkv, qkv, kaug, causal)


def kernel(x, ffn1_gate, ffn1_up, ffn1_down, ln1_g, ln1_b, w_in, w_out, ln2_g, ln2_b,
           ffn2_gate, ffn2_up, ffn2_down, ln3_g, ln3_b):
    assert x.shape[1:] == (SEQ, D_MODEL) and ffn1_gate.shape == (DEPTH, D_MODEL, D_FF)
    bf = lambda w: w[0].astype(BF16)
    row = lambda p: p[0].reshape(1, D_MODEL).astype(F32)
    t12, t3 = _dilated_bias_tables()
    kaug, causal = _moba_tables()
    perm = _tile_permutation()

    h1, qkv = _ffn_qkv(x, bf(ffn1_gate), bf(ffn1_up), bf(ffn1_down), row(ln1_g), row(ln1_b), bf(w_in),
                       jnp.asarray(perm, BF16))
    oa = _dilated_attention(qkv, jnp.asarray(t12), jnp.asarray(t3))
    ob = _moba_attention(qkv, jnp.asarray(kaug), jnp.asarray(causal))
    w_o = bf(w_out)
    return _out_ffn(oa, ob, h1, jnp.asarray(perm.T, BF16), w_o[:WIDTH_A], w_o[WIDTH_A:], row(ln2_g), row(ln2_b),
                    bf(ffn2_gate), bf(ffn2_up), bf(ffn2_down), row(ln3_g), row(ln3_b))
```

```python
import numpy as np
import jax
import jax.numpy as jnp
from jax import lax
from jax.experimental import pallas as pl
from jax.experimental.pallas import tpu as pltpu

D_MODEL = 1024
SEQ = 2048
DEPTH = 1
HEAD_DIM = 64
N_HEADS = 16
N_HEADS_B = 4
N_HEADS_A = 12
WIDTH_A = N_HEADS_A * HEAD_DIM
WIDTH_B = N_HEADS_B * HEAD_DIM
MIX_WIDTH = WIDTH_A + WIDTH_B
DILATED_CONFIGS = ((128, 1), (512, 4), (2048, 16))
MOBA_BLOCK = 256
MOBA_TOPK = 3
D_FF = 2816
LN_EPS = 1e-5
ALPHA = (2.0 * DEPTH) ** 0.25

LANES = 128
SUBLANES = 8
BF16_ROWS = 16
N_CLASSES = 16
CLASS_ROWS = SEQ // N_CLASSES
BAND = 128
PAIRS_PER_STEP = 2
ROW_TILE = 512
TILE_CLASS_ROWS = ROW_TILE // N_CLASSES
SUB_TILE = 256
SUB_CLASS_ROWS = SUB_TILE // N_CLASSES
FF_CHUNK = 256
NEG = -1e30
LOG2E = 1.4426950408889634
VMEM_LIMIT = 56 * 1024 * 1024

F32 = jnp.float32
BF16 = jnp.bfloat16


def _dot(a, b):
    return jnp.dot(a, b, preferred_element_type=F32)


def _dot_nt(a, b, precision=None):
    return lax.dot_general(a, b, (((1,), (1,)), ((), ())), precision=precision,
                           preferred_element_type=F32)


def _layer_norm(z, g, b):
    mu = jnp.mean(z, axis=-1, keepdims=True)
    zc = z - mu
    var = jnp.mean(zc * zc, axis=-1, keepdims=True)
    return zc * lax.rsqrt(var + LN_EPS) * g + b


def _swiglu_steps(get_xb, wg_ref, wu_ref, act_ref):
    def chunk(c):
        cols = slice(c * FF_CHUNK, (c + 1) * FF_CHUNK)
        xb = get_xb()
        g = _dot(xb, wg_ref[:, cols])
        u = _dot(xb, wu_ref[:, cols])
        act_ref[:, cols] = (g / (1.0 + jnp.exp(-g)) * u).astype(BF16)

    return [lambda c=c: chunk(c) for c in range(D_FF // FF_CHUNK)]


def _emit_spread(main_steps, side_steps):
    gap = max(1, len(main_steps) // (len(side_steps) + 1))
    side = list(side_steps)
    for i, step in enumerate(main_steps):
        step()
        if side and (i + 1) % gap == 0:
            side.pop(0)()
    for step in side:
        step()


def _tile_permutation():
    u = np.arange(SUB_TILE)
    perm = np.zeros((SUB_TILE, SUB_TILE), np.float32)
    perm[u, N_CLASSES * (u % SUB_CLASS_ROWS) + u // SUB_CLASS_ROWS] = 1.0
    return perm


def _ffn_qkv_kernel(x_ref, wg_ref, wu_ref, wd_ref, g_ref, b_ref, win_ref, perm_ref,
                    h_ref, qkv_ref, *act_refs):
    def sub_tile(sub):
        rows = slice(sub * SUB_TILE, (sub + 1) * SUB_TILE)
        act_ref = act_refs[sub]
        val = {}

        def norm():
            x = x_ref[0, rows, :]
            y = _dot(act_ref[...], wd_ref[...])
            h = _layer_norm(ALPHA * x + 0.5 * y, g_ref[...], b_ref[...])
            h_ref[0, rows, :] = h
            val["hb"] = h.astype(BF16)

        def reorder():
            val["hb"] = _dot(perm_ref[...], val["hb"]).astype(BF16)

        def project():
            qkv = _dot(val.pop("hb"), win_ref[...])
            for r in range(N_CLASSES):
                qkv_ref[0, r, sub * SUB_CLASS_ROWS:(sub + 1) * SUB_CLASS_ROWS, :] = (
                    qkv[r * SUB_CLASS_ROWS:(r + 1) * SUB_CLASS_ROWS])

        chunks = _swiglu_steps(lambda: x_ref[0, rows, :].astype(BF16), wg_ref, wu_ref, act_ref)
        return chunks, [norm, reorder, project]

    chunks0, (norm0, reorder0, project0) = sub_tile(0)
    chunks1, (norm1, reorder1, project1) = sub_tile(1)
    for step in chunks0:
        step()
    _emit_spread(chunks1, [norm0, reorder0])
    norm1()
    project0()
    reorder1()
    project1()


def _const_spec(shape):
    return pl.BlockSpec(shape, lambda *_: (0,) * len(shape), pipeline_mode=pl.Buffered(1))


def _ffn_qkv(x, wg, wu, wd, g, b, w_in, perm):
    batch = x.shape[0]
    n_tiles = SEQ // ROW_TILE
    h1, qkv = pl.pallas_call(
        _ffn_qkv_kernel,
        grid=(batch, n_tiles),
        in_specs=[
            pl.BlockSpec((1, ROW_TILE, D_MODEL), lambda bi, ti: (bi, ti, 0)),
            _const_spec((D_MODEL, D_FF)), _const_spec((D_MODEL, D_FF)), _const_spec((D_FF, D_MODEL)),
            _const_spec((1, D_MODEL)), _const_spec((1, D_MODEL)),
            _const_spec((D_MODEL, 3 * MIX_WIDTH)), _const_spec((SUB_TILE, SUB_TILE)),
        ],
        out_specs=[
            pl.BlockSpec((1, ROW_TILE, D_MODEL), lambda bi, ti: (bi, ti, 0)),
            pl.BlockSpec((1, N_CLASSES, TILE_CLASS_ROWS, 3 * MIX_WIDTH), lambda bi, ti: (bi, 0, ti, 0)),
        ],
        out_shape=[
            jax.ShapeDtypeStruct((batch, SEQ, D_MODEL), F32),
            jax.ShapeDtypeStruct((batch, N_CLASSES, CLASS_ROWS, 3 * MIX_WIDTH), F32),
        ],
        scratch_shapes=[pltpu.VMEM((SUB_TILE, D_FF), BF16)] * (ROW_TILE // SUB_TILE),
        compiler_params=pltpu.CompilerParams(
            dimension_semantics=("parallel", "parallel"), vmem_limit_bytes=VMEM_LIMIT),
        name="ffn1_ln1_qkv",
    )(x, wg, wu, wd, g, b, w_in, perm)
    return h1, qkv.reshape(batch, SEQ, 3 * MIX_WIDTH)


def _out_ffn_kernel(oa_ref, ob_ref, h_ref, unperm_ref, woa_ref, wob_ref, g2_ref, b2_ref,
                    wg_ref, wu_ref, wd_ref, g3_ref, b3_ref, out_ref, *act_refs):
    def natural_rows(o_ref, sub):
        o_res = jnp.concatenate(
            [o_ref[0, r, sub * SUB_CLASS_ROWS:(sub + 1) * SUB_CLASS_ROWS, :] for r in range(N_CLASSES)], axis=0)
        return _dot(unperm_ref[...], o_res).astype(BF16)

    def sub_tile(sub):
        rows = slice(sub * SUB_TILE, (sub + 1) * SUB_TILE)
        act_ref = act_refs[sub]
        val = {}

        def mix():
            val["mix"] = (_dot(natural_rows(oa_ref, sub), woa_ref[...])
                          + _dot(natural_rows(ob_ref, sub), wob_ref[...]))

        def norm():
            val["h2"] = _layer_norm(ALPHA * h_ref[0, rows, :] + val.pop("mix"), g2_ref[...], b2_ref[...])
            val["h2b"] = val["h2"].astype(BF16)

        def ffn_norm():
            y = _dot(act_ref[...], wd_ref[...])
            out_ref[0, rows, :] = _layer_norm(ALPHA * val.pop("h2") + 0.5 * y, g3_ref[...], b3_ref[...])

        chunks = _swiglu_steps(lambda: val["h2b"], wg_ref, wu_ref, act_ref)
        return mix, norm, chunks, ffn_norm

    mix0, norm0, chunks0, tail0 = sub_tile(0)
    mix1, norm1, chunks1, tail1 = sub_tile(1)
    mix0()
    mix1()
    norm0()
    _emit_spread(chunks0, [norm1])
    _emit_spread(chunks1, [tail0])
    tail1()


def _out_ffn(oa, ob, h1, unperm, woa, wob, g2, b2, wg, wu, wd, g3, b3):
    batch = h1.shape[0]
    n_tiles = SEQ // ROW_TILE
    res_spec = lambda width: pl.BlockSpec((1, N_CLASSES, TILE_CLASS_ROWS, width),
                                          lambda bi, ti: (bi, 0, ti, 0))
    return pl.pallas_call(
        _out_ffn_kernel,
        grid=(batch, n_tiles),
        in_specs=[
            res_spec(WIDTH_A), res_spec(WIDTH_B),
            pl.BlockSpec((1, ROW_TILE, D_MODEL), lambda bi, ti: (bi, ti, 0)),
            _const_spec((SUB_TILE, SUB_TILE)),
            _const_spec((WIDTH_A, D_MODEL)), _const_spec((WIDTH_B, D_MODEL)),
            _const_spec((1, D_MODEL)), _const_spec((1, D_MODEL)),
            _const_spec((D_MODEL, D_FF)), _const_spec((D_MODEL, D_FF)), _const_spec((D_FF, D_MODEL)),
            _const_spec((1, D_MODEL)), _const_spec((1, D_MODEL)),
        ],
        out_specs=pl.BlockSpec((1, ROW_TILE, D_MODEL), lambda bi, ti: (bi, ti, 0)),
        out_shape=jax.ShapeDtypeStruct((batch, SEQ, D_MODEL), F32),
        scratch_shapes=[pltpu.VMEM((SUB_TILE, D_FF), BF16)] * (ROW_TILE // SUB_TILE),
        compiler_params=pltpu.CompilerParams(
            dimension_semantics=("parallel", "parallel"), vmem_limit_bytes=VMEM_LIMIT),
        name="outproj_ln2_ffn2_ln3",
    )(oa.reshape(batch, N_CLASSES, CLASS_ROWS, WIDTH_A), ob.reshape(batch, N_CLASSES, CLASS_ROWS, WIDTH_B),
      h1, unperm, woa, wob, g2, b2, wg, wu, wd, g3, b3)


def _alibi_slopes():
    idx = np.arange(N_HEADS)
    slopes = 2.0 ** (-8.0 * (idx + 1) / N_HEADS)
    is_b = (idx % 4) == 3
    return slopes[~is_b], slopes[is_b]


def _lane_is_head0(shape):
    return lax.broadcasted_iota(jnp.int32, shape, 1) < HEAD_DIM


def _gather_rows(ref, pieces):
    return jnp.concatenate([ref[0, s:s + n, :] for s, n in pieces], axis=0)


def _dilated_pieces(branch, blk):
    if branch == 0:
        return [(CLASS_ROWS * r + 8 * blk, 8) for r in range(N_CLASSES)]
    if branch == 1:
        r4, b = divmod(blk, 4)
        return [(CLASS_ROWS * (r4 + 4 * c) + 32 * b, 32) for c in range(4)]
    return [(CLASS_ROWS * blk, CLASS_ROWS)]


def _dilated_local_pos():
    u = np.arange(BAND)
    return [16 * (u % 8) + u // 8, 4 * (u % 32) + u // 32, u]


def _dilated_bias_tables():
    slopes_a, _ = _alibi_slopes()
    pos = _dilated_local_pos()
    t12 = np.zeros((N_HEADS_A, 2, BAND, 2 * BAND), np.float64)
    t3 = np.zeros((N_HEADS_A, BAND, BAND), np.float64)
    for br in range(2):
        dil = DILATED_CONFIGS[br][1]
        a_q = pos[br][:, None]
        j_band = np.concatenate([pos[br], BAND + pos[br]])[None, :]
        delta = a_q + BAND - j_band
        valid = (delta >= 0) & (delta <= BAND)
        for h in range(N_HEADS_A):
            t12[h, br] = np.where(valid, -LOG2E * slopes_a[h] * delta * dil, NEG)
    dil = DILATED_CONFIGS[2][1]
    delta = pos[2][:, None] - pos[2][None, :]
    for h in range(N_HEADS_A):
        t3[h] = np.where(delta >= 0, -LOG2E * slopes_a[h] * delta * dil, NEG)
    t12 = t12.reshape(N_HEADS_A // 2, 2, 2, BAND, 2 * BAND).transpose(0, 2, 1, 3, 4)
    return (t12.reshape(N_HEADS_A // 2, 2, 2 * BAND, 2 * BAND).astype(np.float32),
            t3.reshape(N_HEADS_A // 2, 2 * BAND, BAND).astype(np.float32))


def _dilated_kernel(q_ref, k_ref, v_ref, t12_ref, t3_ref, o_ref,
                    q2_ref, ks_ref, vs_ref, acc_ref, m_ref, l_ref):
    n_blocks = SEQ // BAND
    head0 = _lane_is_head0((BAND, LANES))
    ones = jnp.ones((2 * BAND, LANES), BF16)

    def per_head(x):
        return jnp.where(head0, x[:BAND], x[BAND:])

    def stage(pp, br, blk):
        lanes = slice(LANES * pp, LANES * (pp + 1))
        gather = lambda ref: jnp.concatenate([ref[0, s:s + n, lanes] for s, n in _dilated_pieces(br, blk)], axis=0)
        rows = slice(BAND * blk, BAND * (blk + 1))
        qblk = gather(q_ref) * (LOG2E * HEAD_DIM ** -0.5)
        q2_ref[pp, br, 2 * BAND * blk:2 * BAND * blk + BAND, :] = jnp.where(head0, qblk, 0.0).astype(BF16)
        q2_ref[pp, br, 2 * BAND * blk + BAND:2 * BAND * (blk + 1), :] = jnp.where(head0, 0.0, qblk).astype(BF16)
        ks_ref[pp, br, rows, :] = gather(k_ref).astype(BF16)
        vs_ref[pp, br, rows, :] = gather(v_ref).astype(BF16)

    def block_steps(pp, br, blk):
        first = (blk == 0) if br == 0 else (blk % 4 == 0) if br == 1 else True
        lo = BAND * blk if first else BAND * (blk - 1)
        keys = slice(lo, BAND * (blk + 1))
        n_keys = BAND * (blk + 1) - lo
        val = {}

        def scores():
            table = t3_ref[pp] if br == 2 else t12_ref[pp, br, :, 2 * BAND - n_keys:]
            s2 = _dot_nt(q2_ref[pp, br, 2 * BAND * blk:2 * BAND * (blk + 1), :], ks_ref[pp, br, keys, :]) + table
            val["s"], val["m"] = s2, jnp.max(s2, axis=1, keepdims=True)

        def weigh():
            p = jnp.exp2(val.pop("s") - val["m"]).astype(BF16)
            val["pv"] = _dot(p, jnp.concatenate([vs_ref[pp, br, keys, :], ones[:n_keys]], axis=1))

        def results():
            pv = val.pop("pv")
            acc, l = per_head(pv[:, :LANES]), per_head(pv[:, LANES:])
            m = per_head(jnp.broadcast_to(val.pop("m"), (2 * BAND, LANES)))
            if br < 2:
                off = 0
                for s, n in _dilated_pieces(br, blk):
                    acc_ref[pp, br, s:s + n, :] = acc[off:off + n]
                    m_ref[pp, br, s:s + n, :] = m[off:off + n]
                    l_ref[pp, br, s:s + n, :] = l[off:off + n]
                    off += n
                return
            rows = slice(CLASS_ROWS * blk, CLASS_ROWS * (blk + 1))
            m1, m2 = m_ref[pp, 0, rows, :], m_ref[pp, 1, rows, :]
            m_all = jnp.maximum(jnp.maximum(m1, m2), m)
            w1, w2, w3 = jnp.exp2(m1 - m_all), jnp.exp2(m2 - m_all), jnp.exp2(m - m_all)
            num = w1 * acc_ref[pp, 0, rows, :] + w2 * acc_ref[pp, 1, rows, :] + w3 * acc
            den = w1 * l_ref[pp, 0, rows, :] + w2 * l_ref[pp, 1, rows, :] + w3 * l
            o_ref[0, rows, LANES * pp:LANES * (pp + 1)] = (num / den).astype(BF16)

        return scores, weigh, results

    def pair_steps(pp):
        blocks = [block_steps(pp, br, blk) for br in range(3) for blk in range(n_blocks)]
        depth = 3

        def iteration(i):
            if i < 2 * n_blocks:
                stage(pp, 1 + i // n_blocks, i % n_blocks)
            for d in range(depth):
                if 0 <= i - d < len(blocks):
                    blocks[i - d][d]()

        first = [lambda blk=blk: stage(pp, 0, blk) for blk in range(n_blocks)]
        return first + [lambda i=i: iteration(i) for i in range(len(blocks) + depth - 1)]

    steps = [pair_steps(pp) for pp in range(PAIRS_PER_STEP)]
    offset = len(steps[0]) // 2
    for i in range(len(steps[0]) + offset * (PAIRS_PER_STEP - 1)):
        for pp in range(PAIRS_PER_STEP):
            if 0 <= i - offset * pp < len(steps[pp]):
                steps[pp][i - offset * pp]()


def _dilated_attention(qkv, t12, t3):
    batch = qkv.shape[0]
    n_steps = N_HEADS_A // 2 // PAIRS_PER_STEP
    width = PAIRS_PER_STEP * LANES
    col = lambda base: (lambda bi, si: (bi, 0, base + si))
    return pl.pallas_call(
        _dilated_kernel,
        grid=(batch, n_steps),
        in_specs=[
            pl.BlockSpec((1, SEQ, width), col(0)),
            pl.BlockSpec((1, SEQ, width), col(WIDTH_A // width)),
            pl.BlockSpec((1, SEQ, width), col(2 * WIDTH_A // width)),
            pl.BlockSpec((PAIRS_PER_STEP, 2, 2 * BAND, 2 * BAND), lambda bi, si: (si, 0, 0, 0)),
            pl.BlockSpec((PAIRS_PER_STEP, 2 * BAND, BAND), lambda bi, si: (si, 0, 0)),
        ],
        out_specs=pl.BlockSpec((1, SEQ, width), lambda bi, si: (bi, 0, si)),
        out_shape=jax.ShapeDtypeStruct((batch, SEQ, WIDTH_A), BF16),
        scratch_shapes=[
            pltpu.VMEM((PAIRS_PER_STEP, 3, 2 * SEQ, LANES), BF16),
            pltpu.VMEM((PAIRS_PER_STEP, 3, SEQ, LANES), BF16),
            pltpu.VMEM((PAIRS_PER_STEP, 3, SEQ, LANES), BF16),
            pltpu.VMEM((PAIRS_PER_STEP, 2, SEQ, LANES), F32),
            pltpu.VMEM((PAIRS_PER_STEP, 2, SEQ, LANES), F32),
            pltpu.VMEM((PAIRS_PER_STEP, 2, SEQ, LANES), F32),
        ],
        compiler_params=pltpu.CompilerParams(
            dimension_semantics=("parallel", "parallel"), vmem_limit_bytes=VMEM_LIMIT),
        name="dilated_attention",
    )(qkv, qkv, qkv, t12, t3)


N_MOBA_BLOCKS = SEQ // MOBA_BLOCK
MOBA_PIECE = MOBA_BLOCK // N_CLASSES
SEL_LANES = 2 * N_MOBA_BLOCKS
POS_PIECES = 3
V_ROWS = HEAD_DIM + BF16_ROWS


def _moba_pieces(blk):
    return [(CLASS_ROWS * r + MOBA_PIECE * blk, MOBA_PIECE) for r in range(N_CLASSES)]


def _moba_tables():
    _, slopes_b = _alibi_slopes()
    u = np.arange(MOBA_BLOCK)
    pos = N_CLASSES * (u % MOBA_PIECE) + u // MOBA_PIECE
    kaug = np.zeros((N_HEADS_B, SEQ, LANES), np.float32)
    for h in range(N_HEADS_B):
        for blk in range(N_MOBA_BLOCKS):
            rows = slice(MOBA_BLOCK * blk, MOBA_BLOCK * (blk + 1))
            kaug[h, rows, N_MOBA_BLOCKS * (h % 2) + blk] = 1.0
            rest = LOG2E * slopes_b[h] * (MOBA_BLOCK * blk + pos).astype(np.float64)
            for piece in range(POS_PIECES):
                part = rest.astype(BF16).astype(np.float64)
                kaug[h, rows, SEL_LANES + piece] = part
                rest = rest - part
    causal = np.where(pos[None, :] >= pos[:, None], 0.0, NEG).astype(np.float32)
    return kaug.astype(BF16), causal


def _moba_pair_steps(q_ref, k_ref, v_ref, kaug_ref, causal_ref, o_ref,
                     qs_ref, ka_ref, kb_ref, vt_ref, s_refs, p_refs):
    head0 = _lane_is_head0((MOBA_BLOCK, LANES))
    n_sel = min(MOBA_TOPK, N_MOBA_BLOCKS - 1)
    ones_rows = jnp.ones((BF16_ROWS, MOBA_BLOCK), BF16)
    blk_id = lax.broadcasted_iota(jnp.int32, (N_MOBA_BLOCKS, MOBA_BLOCK), 0)
    pos_rows = jnp.where(blk_id < POS_PIECES, 1.0, 0.0)
    pad_rows = jnp.zeros((LANES - SEL_LANES - N_MOBA_BLOCKS, MOBA_BLOCK), F32)
    kmean_h = []

    def stage():
        kmean_rows = []
        for blk in range(N_MOBA_BLOCKS):
            pieces = _moba_pieces(blk)
            rows = slice(MOBA_BLOCK * blk, MOBA_BLOCK * (blk + 1))
            qs_ref[rows, :LANES] = (_gather_rows(q_ref, pieces) * (LOG2E * HEAD_DIM ** -0.5)).astype(BF16)
            kblk = _gather_rows(k_ref, pieces)
            ka_ref[rows, :] = jnp.where(head0, kblk, 0.0).astype(BF16)
            kb_ref[rows, :] = jnp.where(head0, 0.0, kblk).astype(BF16)
            vt = _gather_rows(v_ref, pieces).T
            for h in range(2):
                vt_ref[V_ROWS * h:V_ROWS * h + HEAD_DIM, rows] = vt[HEAD_DIM * h:HEAD_DIM * (h + 1)].astype(BF16)
                vt_ref[V_ROWS * h + HEAD_DIM:V_ROWS * (h + 1), rows] = ones_rows
            kmean_rows.append(jnp.sum(kblk, axis=0, keepdims=True) * (1.0 / MOBA_BLOCK))
        kmean = jnp.concatenate(kmean_rows, axis=0)
        head0_k = _lane_is_head0((N_MOBA_BLOCKS, LANES))
        kmean_h.extend([jnp.where(head0_k, kmean, 0.0), jnp.where(head0_k, 0.0, kmean)])

    def query_extra_lanes(n):
        sel = [jnp.zeros((N_MOBA_BLOCKS, MOBA_BLOCK), F32)] * 2
        if n > n_sel:
            q_f32 = _gather_rows(q_ref, _moba_pieces(n))
            for h in range(2):
                gate = _dot_nt(kmean_h[h], q_f32, precision=lax.Precision.HIGHEST)
                for m in range(n):
                    g_m = gate[m:m + 1, :]
                    beats = ((gate > g_m) | ((gate == g_m) & (blk_id < m))) & (blk_id < n)
                    rank = jnp.sum(beats.astype(F32), axis=0, keepdims=True)
                    sel[h] = jnp.where((blk_id == m) & (rank >= n_sel), NEG, sel[h])
        q_extra = jnp.concatenate([sel[0], sel[1], pos_rows, pad_rows], axis=0).T
        qs_ref[MOBA_BLOCK * n:MOBA_BLOCK * (n + 1), LANES:] = q_extra.astype(BF16)

    def score_steps(n, slot, fold):
        def step(m):
            k_rows = slice(MOBA_BLOCK * m, MOBA_BLOCK * (m + 1))
            k_aug = jnp.concatenate(
                [jnp.concatenate([ka_ref[k_rows, :], kaug_ref[0, k_rows, :]], axis=1),
                 jnp.concatenate([kb_ref[k_rows, :], kaug_ref[1, k_rows, :]], axis=1)], axis=0)
            s2 = _dot_nt(k_aug, qs_ref[MOBA_BLOCK * n:MOBA_BLOCK * (n + 1), :])
            for h in range(2):
                s = s2[h * MOBA_BLOCK:(h + 1) * MOBA_BLOCK]
                if m == n:
                    s = s + causal_ref[...]
                s_refs[slot][h, k_rows, :] = s
                parts = [s[SUBLANES * i:SUBLANES * (i + 1)] for i in range(MOBA_BLOCK // SUBLANES)]
                if fold[h] is not None:
                    parts.append(fold[h])
                while len(parts) > 1:
                    parts = [jnp.maximum(a, b) for a, b in zip(parts[::2], parts[1::2])] + parts[len(parts) & ~1:]
                fold[h] = parts[0]

        return [lambda m=m: step(m) for m in range(n + 1)]

    def output_steps(n, slot, fold):
        n_keys = MOBA_BLOCK * (n + 1)
        mx = []

        def probs(m):
            if not mx:
                mx.extend(jnp.max(fold[h], axis=0, keepdims=True) for h in range(2))
            k_rows = slice(MOBA_BLOCK * m, MOBA_BLOCK * (m + 1))
            for h in range(2):
                p_refs[slot][h, k_rows, :] = jnp.exp2(s_refs[slot][h, k_rows, :] - mx[h]).astype(BF16)

        def finish():
            acc = [_dot(vt_ref[V_ROWS * h:V_ROWS * (h + 1), :n_keys], p_refs[slot][h, :n_keys, :])
                   for h in range(2)]
            out_t = jnp.concatenate(
                [acc[h][:HEAD_DIM] / acc[h][HEAD_DIM:HEAD_DIM + 1] for h in range(2)], axis=0)
            out = out_t.T.astype(BF16)
            off = 0
            for s, cnt in _moba_pieces(n):
                o_ref[0, s:s + cnt, :] = out[off:off + cnt]
                off += cnt

        return [lambda m=m: probs(m) for m in range(n + 1)] + [finish]

    order = list(range(N_MOBA_BLOCKS))
    late = [n for n in order if n > n_sel]
    emit = [stage] + [lambda n=n: query_extra_lanes(n) for n in order if n <= n_sel]
    folds = [[None, None] for _ in range(N_MOBA_BLOCKS)]
    for pos in range(N_MOBA_BLOCKS + 1):
        first = score_steps(order[pos], pos % 2, folds[order[pos]]) if pos < N_MOBA_BLOCKS else []
        second = output_steps(order[pos - 1], (pos - 1) % 2, folds[order[pos - 1]]) if pos > 0 else []
        third = [lambda n=n: query_extra_lanes(n) for n in late[2 * pos:2 * pos + 2]]
        for i in range(max(len(first), len(second), len(third))):
            emit.extend(steps[i] for steps in (first, second, third) if i < len(steps))
    return emit


def _moba_kernel(q_ref, k_ref, v_ref, kaug_ref, causal_ref, o_ref,
                 qs_ref, ka_ref, kb_ref, vt_ref, s0_ref, s1_ref, p0_ref, p1_ref):
    steps = []
    for pp in range(N_HEADS_B // 2):
        lanes = slice(LANES * pp, LANES * (pp + 1))
        steps.append(_moba_pair_steps(
            q_ref.at[:, :, lanes], k_ref.at[:, :, lanes], v_ref.at[:, :, lanes],
            kaug_ref.at[2 * pp:2 * pp + 2], causal_ref, o_ref.at[:, :, lanes],
            qs_ref.at[pp], ka_ref.at[pp], kb_ref.at[pp], vt_ref.at[pp],
            (s0_ref.at[pp], s1_ref.at[pp]), (p0_ref.at[pp], p1_ref.at[pp])))
    offset = len(steps[0]) // 2
    for i in range(len(steps[0]) + offset):
        if i < len(steps[0]):
            steps[0][i]()
        if 0 <= i - offset < len(steps[1]):
            steps[1][i - offset]()


def _moba_attention(qkv, kaug, causal):
    batch = qkv.shape[0]
    n_pairs = N_HEADS_B // 2
    col = lambda base: (lambda bi: (bi, 0, base))
    base_q = 3 * WIDTH_A // WIDTH_B
    return pl.pallas_call(
        _moba_kernel,
        grid=(batch,),
        in_specs=[
            pl.BlockSpec((1, SEQ, WIDTH_B), col(base_q)),
            pl.BlockSpec((1, SEQ, WIDTH_B), col(base_q + 1)),
            pl.BlockSpec((1, SEQ, WIDTH_B), col(base_q + 2)),
            pl.BlockSpec((N_HEADS_B, SEQ, LANES), lambda bi: (0, 0, 0)),
            pl.BlockSpec((MOBA_BLOCK, MOBA_BLOCK), lambda bi: (0, 0)),
        ],
        out_specs=pl.BlockSpec((1, SEQ, WIDTH_B), lambda bi: (bi, 0, 0)),
        out_shape=jax.ShapeDtypeStruct((batch, SEQ, WIDTH_B), BF16),
        scratch_shapes=[
            pltpu.VMEM((n_pairs, SEQ, 2 * LANES), BF16),
            pltpu.VMEM((n_pairs, SEQ, LANES), BF16),
            pltpu.VMEM((n_pairs, SEQ, LANES), BF16),
            pltpu.VMEM((n_pairs, 2 * V_ROWS, SEQ), BF16),
            pltpu.VMEM((n_pairs, 2, SEQ, MOBA_BLOCK), F32),
            pltpu.VMEM((n_pairs, 2, SEQ, MOBA_BLOCK), F32),
            pltpu.VMEM((n_pairs, 2, SEQ, MOBA_BLOCK), BF16),
            pltpu.VMEM((n_pairs, 2, SEQ, MOBA_BLOCK), BF16),
        ],
        compiler_params=pltpu.CompilerParams(
            dimension_semantics=("parallel",), vmem_limit_bytes=VMEM_LIMIT),
        name="moba_attention",
    )(qkv, qkv, qkv, kaug, causal)


def kernel(x, ffn1_gate, ffn1_up, ffn1_down, ln1_g, ln1_b, w_in, w_out, ln2_g, ln2_b,
           ffn2_gate, ffn2_up, ffn2_down, ln3_g, ln3_b):
    assert x.shape[1:] == (SEQ, D_MODEL) and ffn1_gate.shape == (DEPTH, D_MODEL, D_FF)
    bf = lambda w: w[0].astype(BF16)
    row = lambda p: p[0].reshape(1, D_MODEL).astype(F32)
    t12, t3 = _dilated_bias_tables()
    kaug, causal = _moba_tables()
    perm = _tile_permutation()

    h1, qkv = _ffn_qkv(x, bf(ffn1_gate), bf(ffn1_up), bf(ffn1_down), row(ln1_g), row(ln1_b), bf(w_in),
                       jnp.asarray(perm, BF16))
    oa = _dilated_attention(qkv, jnp.asarray(t12), jnp.asarray(t3))
    ob = _moba_attention(qkv, jnp.asarray(kaug), jnp.asarray(causal))
    w_o = bf(w_out)
    return _out_ffn(oa, ob, h1, jnp.asarray(perm.T, BF16), w_o[:WIDTH_A], w_o[WIDTH_A:], row(ln2_g), row(ln2_b),
                    bf(ffn2_gate), bf(ffn2_up), bf(ffn2_down), row(ln3_g), row(ln3_b))
```

```python
import numpy as np
import jax
import jax.numpy as jnp
from jax import lax
from jax.experimental import pallas as pl
from jax.experimental.pallas import tpu as pltpu

D_MODEL = 1024
SEQ = 2048
DEPTH = 1
HEAD_DIM = 64
N_HEADS = 16
N_HEADS_B = 4
N_HEADS_A = 12
WIDTH_A = N_HEADS_A * HEAD_DIM
WIDTH_B = N_HEADS_B * HEAD_DIM
MIX_WIDTH = WIDTH_A + WIDTH_B
DILATED_CONFIGS = ((128, 1), (512, 4), (2048, 16))
MOBA_BLOCK = 256
MOBA_TOPK = 3
D_FF = 2816
LN_EPS = 1e-5
ALPHA = (2.0 * DEPTH) ** 0.25

LANES = 128
QKV_GROUPS = 3 * MIX_WIDTH // LANES
SUBLANES = 8
BF16_ROWS = 16
N_CLASSES = 16
CLASS_ROWS = SEQ // N_CLASSES
BAND = 128
PAIRS_PER_STEP = 2
ROW_TILE = 512
TILE_CLASS_ROWS = ROW_TILE // N_CLASSES
SUB_TILE = 256
SUB_CLASS_ROWS = SUB_TILE // N_CLASSES
FF_CHUNK = 256
NEG = -1e30
LOG2E = 1.4426950408889634
VMEM_LIMIT = 56 * 1024 * 1024

F32 = jnp.float32
BF16 = jnp.bfloat16


def _dot(a, b):
    return jnp.dot(a, b, preferred_element_type=F32)


def _dot_nt(a, b, precision=None):
    return lax.dot_general(a, b, (((1,), (1,)), ((), ())), precision=precision,
                           preferred_element_type=F32)


def _layer_norm(z, g, b):
    mu = jnp.mean(z, axis=-1, keepdims=True)
    zc = z - mu
    var = jnp.mean(zc * zc, axis=-1, keepdims=True)
    return zc * lax.rsqrt(var + LN_EPS) * g + b


def _swiglu_steps(get_xb, wg_ref, wu_ref, act_ref):
    def chunk(c):
        cols = slice(c * FF_CHUNK, (c + 1) * FF_CHUNK)
        xb = get_xb()
        g = _dot(xb, wg_ref[:, cols])
        u = _dot(xb, wu_ref[:, cols])
        act_ref[:, cols] = (g / (1.0 + jnp.exp(-g)) * u).astype(BF16)

    return [lambda c=c: chunk(c) for c in range(D_FF // FF_CHUNK)]


def _emit_spread(main_steps, side_steps):
    gap = max(1, len(main_steps) // (len(side_steps) + 1))
    side = list(side_steps)
    for i, step in enumerate(main_steps):
        step()
        if side and (i + 1) % gap == 0:
            side.pop(0)()
    for step in side:
        step()


def _tile_permutation():
    u = np.arange(SUB_TILE)
    perm = np.zeros((SUB_TILE, SUB_TILE), np.float32)
    perm[u, N_CLASSES * (u % SUB_CLASS_ROWS) + u // SUB_CLASS_ROWS] = 1.0
    return perm


def _ffn_qkv_kernel(x_ref, wg_ref, wu_ref, wd_ref, g_ref, b_ref, win_ref, perm_ref,
                    h_ref, qkv_ref, *act_refs):
    def sub_tile(sub):
        rows = slice(sub * SUB_TILE, (sub + 1) * SUB_TILE)
        act_ref = act_refs[sub]
        val = {}

        def norm():
            x = x_ref[0, rows, :]
            y = _dot(act_ref[...], wd_ref[...])
            h = _layer_norm(ALPHA * x + 0.5 * y, g_ref[...], b_ref[...])
            h_ref[0, rows, :] = h
            val["hb"] = h.astype(BF16)

        def reorder():
            val["hb"] = _dot(perm_ref[...], val["hb"]).astype(BF16)

        def project():
            qkv = _dot(val.pop("hb"), win_ref[...])
            for c in range(QKV_GROUPS):
                for r in range(N_CLASSES):
                    qkv_ref[0, c, r, sub * SUB_CLASS_ROWS:(sub + 1) * SUB_CLASS_ROWS, :] = (
                        qkv[r * SUB_CLASS_ROWS:(r + 1) * SUB_CLASS_ROWS, c * LANES:(c + 1) * LANES])

        chunks = _swiglu_steps(lambda: x_ref[0, rows, :].astype(BF16), wg_ref, wu_ref, act_ref)
        return chunks, [norm, reorder, project]

    chunks0, (norm0, reorder0, project0) = sub_tile(0)
    chunks1, (norm1, reorder1, project1) = sub_tile(1)
    for step in chunks0:
        step()
    _emit_spread(chunks1, [norm0, reorder0])
    norm1()
    project0()
    reorder1()
    project1()


def _const_spec(shape):
    return pl.BlockSpec(shape, lambda *_: (0,) * len(shape), pipeline_mode=pl.Buffered(1))


def _ffn_qkv(x, wg, wu, wd, g, b, w_in, perm):
    batch = x.shape[0]
    n_tiles = SEQ // ROW_TILE
    h1, qkv = pl.pallas_call(
        _ffn_qkv_kernel,
        grid=(batch, n_tiles),
        in_specs=[
            pl.BlockSpec((1, ROW_TILE, D_MODEL), lambda bi, ti: (bi, ti, 0)),
            _const_spec((D_MODEL, D_FF)), _const_spec((D_MODEL, D_FF)), _const_spec((D_FF, D_MODEL)),
            _const_spec((1, D_MODEL)), _const_spec((1, D_MODEL)),
            _const_spec((D_MODEL, 3 * MIX_WIDTH)), _const_spec((SUB_TILE, SUB_TILE)),
        ],
        out_specs=[
            pl.BlockSpec((1, ROW_TILE, D_MODEL), lambda bi, ti: (bi, ti, 0)),
            pl.BlockSpec((1, QKV_GROUPS, N_CLASSES, TILE_CLASS_ROWS, LANES), lambda bi, ti: (bi, 0, 0, ti, 0)),
        ],
        out_shape=[
            jax.ShapeDtypeStruct((batch, SEQ, D_MODEL), F32),
            jax.ShapeDtypeStruct((batch, QKV_GROUPS, N_CLASSES, CLASS_ROWS, LANES), F32),
        ],
        scratch_shapes=[pltpu.VMEM((SUB_TILE, D_FF), BF16)] * (ROW_TILE // SUB_TILE),
        compiler_params=pltpu.CompilerParams(
            dimension_semantics=("parallel", "parallel"), vmem_limit_bytes=VMEM_LIMIT),
        name="ffn1_ln1_qkv",
    )(x, wg, wu, wd, g, b, w_in, perm)
    return h1, qkv.reshape(batch, QKV_GROUPS, SEQ, LANES)


def _out_ffn_kernel(oa_ref, ob_ref, h_ref, unperm_ref, woa_ref, wob_ref, g2_ref, b2_ref,
                    wg_ref, wu_ref, wd_ref, g3_ref, b3_ref, out_ref, *act_refs):
    def natural_rows(o_ref, sub):
        part = slice(sub * SUB_CLASS_ROWS, (sub + 1) * SUB_CLASS_ROWS)
        o_res = jnp.concatenate(
            [jnp.concatenate([o_ref[0, p, r, part, :] for p in range(o_ref.shape[1])], axis=1)
             for r in range(N_CLASSES)], axis=0)
        return _dot(unperm_ref[...], o_res).astype(BF16)

    def sub_tile(sub):
        rows = slice(sub * SUB_TILE, (sub + 1) * SUB_TILE)
        act_ref = act_refs[sub]
        val = {}

        def mix():
            val["mix"] = (_dot(natural_rows(oa_ref, sub), woa_ref[...])
                          + _dot(natural_rows(ob_ref, sub), wob_ref[...]))

        def norm():
            val["h2"] = _layer_norm(ALPHA * h_ref[0, rows, :] + val.pop("mix"), g2_ref[...], b2_ref[...])
            val["h2b"] = val["h2"].astype(BF16)

        def ffn_norm():
            y = _dot(act_ref[...], wd_ref[...])
            out_ref[0, rows, :] = _layer_norm(ALPHA * val.pop("h2") + 0.5 * y, g3_ref[...], b3_ref[...])

        chunks = _swiglu_steps(lambda: val["h2b"], wg_ref, wu_ref, act_ref)
        return mix, norm, chunks, ffn_norm

    mix0, norm0, chunks0, tail0 = sub_tile(0)
    mix1, norm1, chunks1, tail1 = sub_tile(1)
    mix0()
    mix1()
    norm0()
    _emit_spread(chunks0, [norm1])
    _emit_spread(chunks1, [tail0])
    tail1()


def _out_ffn(oa, ob, h1, unperm, woa, wob, g2, b2, wg, wu, wd, g3, b3):
    batch = h1.shape[0]
    n_tiles = SEQ // ROW_TILE
    res_spec = lambda width: pl.BlockSpec((1, width // LANES, N_CLASSES, TILE_CLASS_ROWS, LANES),
                                          lambda bi, ti: (bi, 0, 0, ti, 0))
    return pl.pallas_call(
        _out_ffn_kernel,
        grid=(batch, n_tiles),
        in_specs=[
            res_spec(WIDTH_A), res_spec(WIDTH_B),
            pl.BlockSpec((1, ROW_TILE, D_MODEL), lambda bi, ti: (bi, ti, 0)),
            _const_spec((SUB_TILE, SUB_TILE)),
            _const_spec((WIDTH_A, D_MODEL)), _const_spec((WIDTH_B, D_MODEL)),
            _const_spec((1, D_MODEL)), _const_spec((1, D_MODEL)),
            _const_spec((D_MODEL, D_FF)), _const_spec((D_MODEL, D_FF)), _const_spec((D_FF, D_MODEL)),
            _const_spec((1, D_MODEL)), _const_spec((1, D_MODEL)),
        ],
        out_specs=pl.BlockSpec((1, ROW_TILE, D_MODEL), lambda bi, ti: (bi, ti, 0)),
        out_shape=jax.ShapeDtypeStruct((batch, SEQ, D_MODEL), F32),
        scratch_shapes=[pltpu.VMEM((SUB_TILE, D_FF), BF16)] * (ROW_TILE // SUB_TILE),
        compiler_params=pltpu.CompilerParams(
            dimension_semantics=("parallel", "parallel"), vmem_limit_bytes=VMEM_LIMIT),
        name="outproj_ln2_ffn2_ln3",
    )(oa.reshape(batch, WIDTH_A // LANES, N_CLASSES, CLASS_ROWS, LANES),
      ob.reshape(batch, WIDTH_B // LANES, N_CLASSES, CLASS_ROWS, LANES),
      h1, unperm, woa, wob, g2, b2, wg, wu, wd, g3, b3)


def _alibi_slopes():
    idx = np.arange(N_HEADS)
    slopes = 2.0 ** (-8.0 * (idx + 1) / N_HEADS)
    is_b = (idx % 4) == 3
    return slopes[~is_b], slopes[is_b]


def _lane_is_head0(shape):
    return lax.broadcasted_iota(jnp.int32, shape, 1) < HEAD_DIM


def _gather_rows(ref, pieces):
    return jnp.concatenate([ref[0, s:s + n, :] for s, n in pieces], axis=0)


def _dilated_pieces(branch, blk):
    if branch == 0:
        return [(CLASS_ROWS * r + 8 * blk, 8) for r in range(N_CLASSES)]
    if branch == 1:
        r4, b = divmod(blk, 4)
        return [(CLASS_ROWS * (r4 + 4 * c) + 32 * b, 32) for c in range(4)]
    return [(CLASS_ROWS * blk, CLASS_ROWS)]


def _dilated_local_pos():
    u = np.arange(BAND)
    return [16 * (u % 8) + u // 8, 4 * (u % 32) + u // 32, u]


def _dilated_bias_tables():
    slopes_a, _ = _alibi_slopes()
    pos = _dilated_local_pos()
    t12 = np.zeros((N_HEADS_A, 2, BAND, 2 * BAND), np.float64)
    t3 = np.zeros((N_HEADS_A, BAND, BAND), np.float64)
    for br in range(2):
        dil = DILATED_CONFIGS[br][1]
        a_q = pos[br][:, None]
        j_band = np.concatenate([pos[br], BAND + pos[br]])[None, :]
        delta = a_q + BAND - j_band
        valid = (delta >= 0) & (delta <= BAND)
        for h in range(N_HEADS_A):
            t12[h, br] = np.where(valid, -LOG2E * slopes_a[h] * delta * dil, NEG)
    dil = DILATED_CONFIGS[2][1]
    delta = pos[2][:, None] - pos[2][None, :]
    for h in range(N_HEADS_A):
        t3[h] = np.where(delta >= 0, -LOG2E * slopes_a[h] * delta * dil, NEG)
    t12 = t12.reshape(N_HEADS_A // 2, 2, 2, BAND, 2 * BAND).transpose(0, 2, 1, 3, 4)
    return (t12.reshape(N_HEADS_A // 2, 2, 2 * BAND, 2 * BAND).astype(np.float32),
            t3.reshape(N_HEADS_A // 2, 2 * BAND, BAND).astype(np.float32))


def _dilated_kernel(q_ref, k_ref, v_ref, t12_ref, t3_ref, o_ref,
                    q2_ref, ks_ref, vs_ref, acc_ref, m_ref, l_ref):
    n_blocks = SEQ // BAND
    head0 = _lane_is_head0((BAND, LANES))
    ones = jnp.ones((2 * BAND, LANES), BF16)

    def per_head(x):
        return jnp.where(head0, x[:BAND], x[BAND:])

    def stage(pp, br, blk):
        gather = lambda ref: jnp.concatenate([ref[0, pp, s:s + n, :] for s, n in _dilated_pieces(br, blk)], axis=0)
        rows = slice(BAND * blk, BAND * (blk + 1))
        qblk = gather(q_ref) * (LOG2E * HEAD_DIM ** -0.5)
        q2_ref[pp, br, 2 * BAND * blk:2 * BAND * blk + BAND, :] = jnp.where(head0, qblk, 0.0).astype(BF16)
        q2_ref[pp, br, 2 * BAND * blk + BAND:2 * BAND * (blk + 1), :] = jnp.where(head0, 0.0, qblk).astype(BF16)
        ks_ref[pp, br, rows, :] = gather(k_ref).astype(BF16)
        vs_ref[pp, br, rows, :] = gather(v_ref).astype(BF16)

    def block_steps(pp, br, blk):
        first = (blk == 0) if br == 0 else (blk % 4 == 0) if br == 1 else True
        lo = BAND * blk if first else BAND * (blk - 1)
        keys = slice(lo, BAND * (blk + 1))
        n_keys = BAND * (blk + 1) - lo
        val = {}

        def scores():
            table = t3_ref[pp] if br == 2 else t12_ref[pp, br, :, 2 * BAND - n_keys:]
            s2 = _dot_nt(q2_ref[pp, br, 2 * BAND * blk:2 * BAND * (blk + 1), :], ks_ref[pp, br, keys, :]) + table
            val["s"], val["m"] = s2, jnp.max(s2, axis=1, keepdims=True)

        def weigh():
            p = jnp.exp2(val.pop("s") - val["m"]).astype(BF16)
            val["pv"] = _dot(p, jnp.concatenate([vs_ref[pp, br, keys, :], ones[:n_keys]], axis=1))

        def results():
            pv = val.pop("pv")
            acc, l = per_head(pv[:, :LANES]), per_head(pv[:, LANES:])
            m = per_head(jnp.broadcast_to(val.pop("m"), (2 * BAND, LANES)))
            if br < 2:
                off = 0
                for s, n in _dilated_pieces(br, blk):
                    acc_ref[pp, br, s:s + n, :] = acc[off:off + n]
                    m_ref[pp, br, s:s + n, :] = m[off:off + n]
                    l_ref[pp, br, s:s + n, :] = l[off:off + n]
                    off += n
                return
            rows = slice(CLASS_ROWS * blk, CLASS_ROWS * (blk + 1))
            m1, m2 = m_ref[pp, 0, rows, :], m_ref[pp, 1, rows, :]
            m_all = jnp.maximum(jnp.maximum(m1, m2), m)
            w1, w2, w3 = jnp.exp2(m1 - m_all), jnp.exp2(m2 - m_all), jnp.exp2(m - m_all)
            num = w1 * acc_ref[pp, 0, rows, :] + w2 * acc_ref[pp, 1, rows, :] + w3 * acc
            den = w1 * l_ref[pp, 0, rows, :] + w2 * l_ref[pp, 1, rows, :] + w3 * l
            o_ref[0, pp, rows, :] = (num / den).astype(BF16)

        return scores, weigh, results

    def pair_steps(pp):
        blocks = [block_steps(pp, br, blk) for br in range(3) for blk in range(n_blocks)]
        depth = 3

        def iteration(i):
            if i < 2 * n_blocks:
                stage(pp, 1 + i // n_blocks, i % n_blocks)
            for d in range(depth):
                if 0 <= i - d < len(blocks):
                    blocks[i - d][d]()

        first = [lambda blk=blk: stage(pp, 0, blk) for blk in range(n_blocks)]
        return first + [lambda i=i: iteration(i) for i in range(len(blocks) + depth - 1)]

    steps = [pair_steps(pp) for pp in range(PAIRS_PER_STEP)]
    offset = len(steps[0]) // 2
    for i in range(len(steps[0]) + offset * (PAIRS_PER_STEP - 1)):
        for pp in range(PAIRS_PER_STEP):
            if 0 <= i - offset * pp < len(steps[pp]):
                steps[pp][i - offset * pp]()


def _dilated_attention(qkv, t12, t3):
    batch = qkv.shape[0]
    n_steps = N_HEADS_A // 2 // PAIRS_PER_STEP
    group = lambda base: pl.BlockSpec((1, PAIRS_PER_STEP, SEQ, LANES), lambda si, bi: (bi, base + si, 0, 0))
    return pl.pallas_call(
        _dilated_kernel,
        grid=(n_steps, batch),
        in_specs=[
            group(0), group(n_steps), group(2 * n_steps),
            pl.BlockSpec((PAIRS_PER_STEP, 2, 2 * BAND, 2 * BAND), lambda si, bi: (si, 0, 0, 0)),
            pl.BlockSpec((PAIRS_PER_STEP, 2 * BAND, BAND), lambda si, bi: (si, 0, 0)),
        ],
        out_specs=group(0),
        out_shape=jax.ShapeDtypeStruct((batch, N_HEADS_A // 2, SEQ, LANES), BF16),
        scratch_shapes=[
            pltpu.VMEM((PAIRS_PER_STEP, 3, 2 * SEQ, LANES), BF16),
            pltpu.VMEM((PAIRS_PER_STEP, 3, SEQ, LANES), BF16),
            pltpu.VMEM((PAIRS_PER_STEP, 3, SEQ, LANES), BF16),
            pltpu.VMEM((PAIRS_PER_STEP, 2, SEQ, LANES), F32),
            pltpu.VMEM((PAIRS_PER_STEP, 2, SEQ, LANES), F32),
            pltpu.VMEM((PAIRS_PER_STEP, 2, SEQ, LANES), F32),
        ],
        compiler_params=pltpu.CompilerParams(
            dimension_semantics=("parallel", "parallel"), vmem_limit_bytes=VMEM_LIMIT),
        name="dilated_attention",
    )(qkv, qkv, qkv, t12, t3)


N_MOBA_BLOCKS = SEQ // MOBA_BLOCK
MOBA_PIECE = MOBA_BLOCK // N_CLASSES
SEL_LANES = 2 * N_MOBA_BLOCKS
POS_PIECES = 3
V_ROWS = HEAD_DIM + BF16_ROWS


def _moba_pieces(blk):
    return [(CLASS_ROWS * r + MOBA_PIECE * blk, MOBA_PIECE) for r in range(N_CLASSES)]


def _moba_tables():
    _, slopes_b = _alibi_slopes()
    u = np.arange(MOBA_BLOCK)
    pos = N_CLASSES * (u % MOBA_PIECE) + u // MOBA_PIECE
    kaug = np.zeros((N_HEADS_B, SEQ, LANES), np.float32)
    for h in range(N_HEADS_B):
        for blk in range(N_MOBA_BLOCKS):
            rows = slice(MOBA_BLOCK * blk, MOBA_BLOCK * (blk + 1))
            kaug[h, rows, N_MOBA_BLOCKS * (h % 2) + blk] = 1.0
            rest = LOG2E * slopes_b[h] * (MOBA_BLOCK * blk + pos).astype(np.float64)
            for piece in range(POS_PIECES):
                part = rest.astype(BF16).astype(np.float64)
                kaug[h, rows, SEL_LANES + piece] = part
                rest = rest - part
    causal = np.where(pos[None, :] >= pos[:, None], 0.0, NEG).astype(np.float32)
    return kaug.astype(BF16), causal


def _moba_pair_steps(q_ref, k_ref, v_ref, kaug_ref, causal_ref, o_ref,
                     qs_ref, ka_ref, kb_ref, vt_ref, s_refs, p_refs):
    head0 = _lane_is_head0((MOBA_BLOCK, LANES))
    n_sel = min(MOBA_TOPK, N_MOBA_BLOCKS - 1)
    ones_rows = jnp.ones((BF16_ROWS, MOBA_BLOCK), BF16)
    blk_id = lax.broadcasted_iota(jnp.int32, (N_MOBA_BLOCKS, MOBA_BLOCK), 0)
    pos_rows = jnp.where(blk_id < POS_PIECES, 1.0, 0.0)
    pad_rows = jnp.zeros((LANES - SEL_LANES - N_MOBA_BLOCKS, MOBA_BLOCK), F32)
    kmean_h = []

    def stage():
        kmean_rows = []
        for blk in range(N_MOBA_BLOCKS):
            pieces = _moba_pieces(blk)
            rows = slice(MOBA_BLOCK * blk, MOBA_BLOCK * (blk + 1))
            qs_ref[rows, :LANES] = (_gather_rows(q_ref, pieces) * (LOG2E * HEAD_DIM ** -0.5)).astype(BF16)
            kblk = _gather_rows(k_ref, pieces)
            ka_ref[rows, :] = jnp.where(head0, kblk, 0.0).astype(BF16)
            kb_ref[rows, :] = jnp.where(head0, 0.0, kblk).astype(BF16)
            vt = _gather_rows(v_ref, pieces).T
            for h in range(2):
                vt_ref[V_ROWS * h:V_ROWS * h + HEAD_DIM, rows] = vt[HEAD_DIM * h:HEAD_DIM * (h + 1)].astype(BF16)
                vt_ref[V_ROWS * h + HEAD_DIM:V_ROWS * (h + 1), rows] = ones_rows
            kmean_rows.append(jnp.sum(kblk, axis=0, keepdims=True) * (1.0 / MOBA_BLOCK))
        kmean = jnp.concatenate(kmean_rows, axis=0)
        head0_k = _lane_is_head0((N_MOBA_BLOCKS, LANES))
        kmean_h.extend([jnp.where(head0_k, kmean, 0.0), jnp.where(head0_k, 0.0, kmean)])

    def query_extra_lanes(n):
        sel = [jnp.zeros((N_MOBA_BLOCKS, MOBA_BLOCK), F32)] * 2
        if n > n_sel:
            q_f32 = _gather_rows(q_ref, _moba_pieces(n))
            for h in range(2):
                gate = _dot_nt(kmean_h[h], q_f32, precision=lax.Precision.HIGHEST)
                for m in range(n):
                    g_m = gate[m:m + 1, :]
                    beats = ((gate > g_m) | ((gate == g_m) & (blk_id < m))) & (blk_id < n)
                    rank = jnp.sum(beats.astype(F32), axis=0, keepdims=True)
                    sel[h] = jnp.where((blk_id == m) & (rank >= n_sel), NEG, sel[h])
        q_extra = jnp.concatenate([sel[0], sel[1], pos_rows, pad_rows], axis=0).T
        qs_ref[MOBA_BLOCK * n:MOBA_BLOCK * (n + 1), LANES:] = q_extra.astype(BF16)

    def score_steps(n, slot, fold):
        def step(m):
            k_rows = slice(MOBA_BLOCK * m, MOBA_BLOCK * (m + 1))
            k_aug = jnp.concatenate(
                [jnp.concatenate([ka_ref[k_rows, :], kaug_ref[0, k_rows, :]], axis=1),
                 jnp.concatenate([kb_ref[k_rows, :], kaug_ref[1, k_rows, :]], axis=1)], axis=0)
            s2 = _dot_nt(k_aug, qs_ref[MOBA_BLOCK * n:MOBA_BLOCK * (n + 1), :])
            for h in range(2):
                s = s2[h * MOBA_BLOCK:(h + 1) * MOBA_BLOCK]
                if m == n:
                    s = s + causal_ref[...]
                s_refs[slot][h, k_rows, :] = s
                parts = [s[SUBLANES * i:SUBLANES * (i + 1)] for i in range(MOBA_BLOCK // SUBLANES)]
                if fold[h] is not None:
                    parts.append(fold[h])
                while len(parts) > 1:
                    parts = [jnp.maximum(a, b) for a, b in zip(parts[::2], parts[1::2])] + parts[len(parts) & ~1:]
                fold[h] = parts[0]

        return [lambda m=m: step(m) for m in range(n + 1)]

    def output_steps(n, slot, fold):
        n_keys = MOBA_BLOCK * (n + 1)
        mx = []

        def probs(m):
            if not mx:
                mx.extend(jnp.max(fold[h], axis=0, keepdims=True) for h in range(2))
            k_rows = slice(MOBA_BLOCK * m, MOBA_BLOCK * (m + 1))
            for h in range(2):
                p_refs[slot][h, k_rows, :] = jnp.exp2(s_refs[slot][h, k_rows, :] - mx[h]).astype(BF16)

        def finish():
            acc = [_dot(vt_ref[V_ROWS * h:V_ROWS * (h + 1), :n_keys], p_refs[slot][h, :n_keys, :])
                   for h in range(2)]
            out_t = jnp.concatenate(
                [acc[h][:HEAD_DIM] / acc[h][HEAD_DIM:HEAD_DIM + 1] for h in range(2)], axis=0)
            out = out_t.T.astype(BF16)
            off = 0
            for s, cnt in _moba_pieces(n):
                o_ref[0, s:s + cnt, :] = out[off:off + cnt]
                off += cnt

        return [lambda m=m: probs(m) for m in range(n + 1)] + [finish]

    order = list(range(N_MOBA_BLOCKS))
    late = [n for n in order if n > n_sel]
    emit = [stage] + [lambda n=n: query_extra_lanes(n) for n in order if n <= n_sel]
    folds = [[None, None] for _ in range(N_MOBA_BLOCKS)]
    for pos in range(N_MOBA_BLOCKS + 1):
        first = score_steps(order[pos], pos % 2, folds[order[pos]]) if pos < N_MOBA_BLOCKS else []
        second = output_steps(order[pos - 1], (pos - 1) % 2, folds[order[pos - 1]]) if pos > 0 else []
        third = [lambda n=n: query_extra_lanes(n) for n in late[2 * pos:2 * pos + 2]]
        for i in range(max(len(first), len(second), len(third))):
            emit.extend(steps[i] for steps in (first, second, third) if i < len(steps))
    return emit


def _moba_kernel(q_ref, k_ref, v_ref, kaug_ref, causal_ref, o_ref,
                 qs_ref, ka_ref, kb_ref, vt_ref, s0_ref, s1_ref, p0_ref, p1_ref):
    steps = []
    for pp in range(N_HEADS_B // 2):
        steps.append(_moba_pair_steps(
            q_ref.at[:, pp], k_ref.at[:, pp], v_ref.at[:, pp],
            kaug_ref.at[2 * pp:2 * pp + 2], causal_ref, o_ref.at[:, pp],
            qs_ref.at[pp], ka_ref.at[pp], kb_ref.at[pp], vt_ref.at[pp],
            (s0_ref.at[pp], s1_ref.at[pp]), (p0_ref.at[pp], p1_ref.at[pp])))
    offset = len(steps[0]) // 2
    for i in range(len(steps[0]) + offset):
        if i < len(steps[0]):
            steps[0][i]()
        if 0 <= i - offset < len(steps[1]):
            steps[1][i - offset]()


def _moba_attention(qkv, kaug, causal):
    batch = qkv.shape[0]
    n_pairs = N_HEADS_B // 2
    group = lambda idx: pl.BlockSpec((1, n_pairs, SEQ, LANES), lambda bi: (bi, idx, 0, 0))
    base_q = 3 * WIDTH_A // WIDTH_B
    return pl.pallas_call(
        _moba_kernel,
        grid=(batch,),
        in_specs=[
            group(base_q), group(base_q + 1), group(base_q + 2),
            pl.BlockSpec((N_HEADS_B, SEQ, LANES), lambda bi: (0, 0, 0)),
            pl.BlockSpec((MOBA_BLOCK, MOBA_BLOCK), lambda bi: (0, 0)),
        ],
        out_specs=group(0),
        out_shape=jax.ShapeDtypeStruct((batch, n_pairs, SEQ, LANES), BF16),
        scratch_shapes=[
            pltpu.VMEM((n_pairs, SEQ, 2 * LANES), BF16),
            pltpu.VMEM((n_pairs, SEQ, LANES), BF16),
            pltpu.VMEM((n_pairs, SEQ, LANES), BF16),
            pltpu.VMEM((n_pairs, 2 * V_ROWS, SEQ), BF16),
            pltpu.VMEM((n_pairs, 2, SEQ, MOBA_BLOCK), F32),
            pltpu.VMEM((n_pairs, 2, SEQ, MOBA_BLOCK), F32),
            pltpu.VMEM((n_pairs, 2, SEQ, MOBA_BLOCK), BF16),
            pltpu.VMEM((n_pairs, 2, SEQ, MOBA_BLOCK), BF16),
        ],
        compiler_params=pltpu.CompilerParams(
            dimension_semantics=("parallel",), vmem_limit_bytes=VMEM_LIMIT),
        name="moba_attention",
    )(qkv, qkv, qkv, kaug, causal)


def kernel(x, ffn1_gate, ffn1_up, ffn1_down, ln1_g, ln1_b, w_in, w_out, ln2_g, ln2_b,
           ffn2_gate, ffn2_up, ffn2_down, ln3_g, ln3_b):
    assert x.shape[1:] == (SEQ, D_MODEL) and ffn1_gate.shape == (DEPTH, D_MODEL, D_FF)
    bf = lambda w: w[0].astype(BF16)
    row = lambda p: p[0].reshape(1, D_MODEL).astype(F32)
    t12, t3 = _dilated_bias_tables()
    kaug, causal = _moba_tables()
    perm = _tile_permutation()

    h1, qkv = _ffn_qkv(x, bf(ffn1_gate), bf(ffn1_up), bf(ffn1_down), row(ln1_g), row(ln1_b), bf(w_in),
                       jnp.asarray(perm, BF16))
    oa = _dilated_attention(qkv, jnp.asarray(t12), jnp.asarray(t3))
    ob = _moba_attention(qkv, jnp.asarray(kaug), jnp.asarray(causal))
    w_o = bf(w_out)
    return _out_ffn(oa, ob, h1, jnp.asarray(perm.T, BF16), w_o[:WIDTH_A], w_o[WIDTH_A:], row(ln2_g), row(ln2_b),
                    bf(ffn2_gate), bf(ffn2_up), bf(ffn2_down), row(ln3_g), row(ln3_b))
```

```python
import numpy as np
import jax
import jax.numpy as jnp
from jax import lax
from jax.experimental import pallas as pl
from jax.experimental.pallas import tpu as pltpu

D_MODEL = 1024
SEQ = 2048
DEPTH = 1
HEAD_DIM = 64
N_HEADS = 16
N_HEADS_B = 4
N_HEADS_A = 12
WIDTH_A = N_HEADS_A * HEAD_DIM
WIDTH_B = N_HEADS_B * HEAD_DIM
MIX_WIDTH = WIDTH_A + WIDTH_B
DILATED_CONFIGS = ((128, 1), (512, 4), (2048, 16))
MOBA_BLOCK = 256
MOBA_TOPK = 3
D_FF = 2816
LN_EPS = 1e-5
ALPHA = (2.0 * DEPTH) ** 0.25

LANES = 128
QKV_GROUPS = 3 * MIX_WIDTH // LANES
SUBLANES = 8
BF16_ROWS = 16
N_CLASSES = 16
CLASS_ROWS = SEQ // N_CLASSES
BAND = 128
PAIRS_PER_STEP = 2
ROW_TILE = 512
OUT_ROW_TILE = 1024
TILE_CLASS_ROWS = ROW_TILE // N_CLASSES
SUB_TILE = 256
SUB_CLASS_ROWS = SUB_TILE // N_CLASSES
FF_CHUNK = 256
NEG = -1e30
LOG2E = 1.4426950408889634
VMEM_LIMIT = 56 * 1024 * 1024

F32 = jnp.float32
BF16 = jnp.bfloat16


def _dot(a, b):
    return jnp.dot(a, b, preferred_element_type=F32)


def _dot_nt(a, b, precision=None):
    return lax.dot_general(a, b, (((1,), (1,)), ((), ())), precision=precision,
                           preferred_element_type=F32)


def _layer_norm(z, g, b):
    mu = jnp.mean(z, axis=-1, keepdims=True)
    zc = z - mu
    var = jnp.mean(zc * zc, axis=-1, keepdims=True)
    return zc * lax.rsqrt(var + LN_EPS) * g + b


def _swiglu_steps(get_xb, wg_ref, wu_ref, act_ref):
    def chunk(c):
        cols = slice(c * FF_CHUNK, (c + 1) * FF_CHUNK)
        xb = get_xb()
        g = _dot(xb, wg_ref[:, cols])
        u = _dot(xb, wu_ref[:, cols])
        act_ref[:, cols] = (g / (1.0 + jnp.exp(-g)) * u).astype(BF16)

    return [lambda c=c: chunk(c) for c in range(D_FF // FF_CHUNK)]


def _emit_spread(main_steps, side_steps):
    gap = max(1, len(main_steps) // (len(side_steps) + 1))
    side = list(side_steps)
    for i, step in enumerate(main_steps):
        step()
        if side and (i + 1) % gap == 0:
            side.pop(0)()
    for step in side:
        step()


def _tile_permutation():
    u = np.arange(SUB_TILE)
    perm = np.zeros((SUB_TILE, SUB_TILE), np.float32)
    perm[u, N_CLASSES * (u % SUB_CLASS_ROWS) + u // SUB_CLASS_ROWS] = 1.0
    return perm


def _ffn_qkv_kernel(x_ref, wg_ref, wu_ref, wd_ref, g_ref, b_ref, win_ref, perm_ref,
                    h_ref, qkv_ref, *act_refs):
    def sub_tile(sub):
        rows = slice(sub * SUB_TILE, (sub + 1) * SUB_TILE)
        act_ref = act_refs[sub]
        val = {}

        def norm():
            x = x_ref[0, rows, :]
            y = _dot(act_ref[...], wd_ref[...])
            h = _layer_norm(ALPHA * x + 0.5 * y, g_ref[...], b_ref[...])
            h_ref[0, rows, :] = h
            val["hb"] = h.astype(BF16)

        def reorder():
            val["hb"] = _dot(perm_ref[...], val["hb"]).astype(BF16)

        def project():
            qkv = _dot(val.pop("hb"), win_ref[...])
            for c in range(QKV_GROUPS):
                for r in range(N_CLASSES):
                    qkv_ref[0, c, r, sub * SUB_CLASS_ROWS:(sub + 1) * SUB_CLASS_ROWS, :] = (
                        qkv[r * SUB_CLASS_ROWS:(r + 1) * SUB_CLASS_ROWS, c * LANES:(c + 1) * LANES])

        chunks = _swiglu_steps(lambda: x_ref[0, rows, :].astype(BF16), wg_ref, wu_ref, act_ref)
        return chunks, [norm, reorder, project]

    chunks0, (norm0, reorder0, project0) = sub_tile(0)
    chunks1, (norm1, reorder1, project1) = sub_tile(1)
    for step in chunks0:
        step()
    _emit_spread(chunks1, [norm0, reorder0])
    norm1()
    project0()
    reorder1()
    project1()


def _const_spec(shape):
    return pl.BlockSpec(shape, lambda *_: (0,) * len(shape), pipeline_mode=pl.Buffered(1))


def _ffn_qkv(x, wg, wu, wd, g, b, w_in, perm):
    batch = x.shape[0]
    n_tiles = SEQ // ROW_TILE
    h1, qkv = pl.pallas_call(
        _ffn_qkv_kernel,
        grid=(batch, n_tiles),
        in_specs=[
            pl.BlockSpec((1, ROW_TILE, D_MODEL), lambda bi, ti: (bi, ti, 0)),
            _const_spec((D_MODEL, D_FF)), _const_spec((D_MODEL, D_FF)), _const_spec((D_FF, D_MODEL)),
            _const_spec((1, D_MODEL)), _const_spec((1, D_MODEL)),
            _const_spec((D_MODEL, 3 * MIX_WIDTH)), _const_spec((SUB_TILE, SUB_TILE)),
        ],
        out_specs=[
            pl.BlockSpec((1, ROW_TILE, D_MODEL), lambda bi, ti: (bi, ti, 0)),
            pl.BlockSpec((1, QKV_GROUPS, N_CLASSES, TILE_CLASS_ROWS, LANES), lambda bi, ti: (bi, 0, 0, ti, 0)),
        ],
        out_shape=[
            jax.ShapeDtypeStruct((batch, SEQ, D_MODEL), F32),
            jax.ShapeDtypeStruct((batch, QKV_GROUPS, N_CLASSES, CLASS_ROWS, LANES), F32),
        ],
        scratch_shapes=[pltpu.VMEM((SUB_TILE, D_FF), BF16)] * (ROW_TILE // SUB_TILE),
        compiler_params=pltpu.CompilerParams(
            dimension_semantics=("parallel", "parallel"), vmem_limit_bytes=VMEM_LIMIT),
        name="ffn1_ln1_qkv",
    )(x, wg, wu, wd, g, b, w_in, perm)
    return h1, qkv.reshape(batch, QKV_GROUPS, SEQ, LANES)


def _out_ffn_kernel(oa_ref, ob_ref, h_ref, unperm_ref, woa_ref, wob_ref, g2_ref, b2_ref,
                    wg_ref, wu_ref, wd_ref, g3_ref, b3_ref, out_ref, *act_refs):
    def natural_rows(o_ref, sub):
        part = slice(sub * SUB_CLASS_ROWS, (sub + 1) * SUB_CLASS_ROWS)
        o_res = jnp.concatenate(
            [jnp.concatenate([o_ref[0, p, r, part, :] for p in range(o_ref.shape[1])], axis=1)
             for r in range(N_CLASSES)], axis=0)
        return _dot(unperm_ref[...], o_res).astype(BF16)

    def sub_tile(sub):
        rows = slice(sub * SUB_TILE, (sub + 1) * SUB_TILE)
        act_ref = act_refs[sub]
        val = {}

        def mix():
            val["mix"] = (_dot(natural_rows(oa_ref, sub), woa_ref[...])
                          + _dot(natural_rows(ob_ref, sub), wob_ref[...]))

        def norm():
            val["h2"] = _layer_norm(ALPHA * h_ref[0, rows, :] + val.pop("mix"), g2_ref[...], b2_ref[...])
            val["h2b"] = val["h2"].astype(BF16)

        def ffn_norm():
            y = _dot(act_ref[...], wd_ref[...])
            out_ref[0, rows, :] = _layer_norm(ALPHA * val.pop("h2") + 0.5 * y, g3_ref[...], b3_ref[...])

        chunks = _swiglu_steps(lambda: val["h2b"], wg_ref, wu_ref, act_ref)
        return mix, norm, chunks, ffn_norm

    subs = [sub_tile(sub) for sub in range(len(act_refs))]
    mix, norm, chunks, tail = zip(*subs)
    mix[0]()
    mix[1]()
    norm[0]()
    for sub in range(len(subs)):
        side = [tail[sub - 1]] if sub > 0 else []
        side += [norm[sub + 1]] if sub + 1 < len(subs) else []
        side += [mix[sub + 2]] if sub + 2 < len(subs) else []
        _emit_spread(chunks[sub], side)
    tail[-1]()


def _out_ffn(oa, ob, h1, unperm, woa, wob, g2, b2, wg, wu, wd, g3, b3):
    batch = h1.shape[0]
    n_tiles = SEQ // OUT_ROW_TILE
    res_spec = lambda width: pl.BlockSpec((1, width // LANES, N_CLASSES, OUT_ROW_TILE // N_CLASSES, LANES),
                                          lambda bi, ti: (bi, 0, 0, ti, 0))
    return pl.pallas_call(
        _out_ffn_kernel,
        grid=(batch, n_tiles),
        in_specs=[
            res_spec(WIDTH_A), res_spec(WIDTH_B),
            pl.BlockSpec((1, OUT_ROW_TILE, D_MODEL), lambda bi, ti: (bi, ti, 0)),
            _const_spec((SUB_TILE, SUB_TILE)),
            _const_spec((WIDTH_A, D_MODEL)), _const_spec((WIDTH_B, D_MODEL)),
            _const_spec((1, D_MODEL)), _const_spec((1, D_MODEL)),
            _const_spec((D_MODEL, D_FF)), _const_spec((D_MODEL, D_FF)), _const_spec((D_FF, D_MODEL)),
            _const_spec((1, D_MODEL)), _const_spec((1, D_MODEL)),
        ],
        out_specs=pl.BlockSpec((1, OUT_ROW_TILE, D_MODEL), lambda bi, ti: (bi, ti, 0)),
        out_shape=jax.ShapeDtypeStruct((batch, SEQ, D_MODEL), F32),
        scratch_shapes=[pltpu.VMEM((SUB_TILE, D_FF), BF16)] * (OUT_ROW_TILE // SUB_TILE),
        compiler_params=pltpu.CompilerParams(
            dimension_semantics=("parallel", "parallel"), vmem_limit_bytes=VMEM_LIMIT),
        name="outproj_ln2_ffn2_ln3",
    )(oa.reshape(batch, WIDTH_A // LANES, N_CLASSES, CLASS_ROWS, LANES),
      ob.reshape(batch, WIDTH_B // LANES, N_CLASSES, CLASS_ROWS, LANES),
      h1, unperm, woa, wob, g2, b2, wg, wu, wd, g3, b3)


def _alibi_slopes():
    idx = np.arange(N_HEADS)
    slopes = 2.0 ** (-8.0 * (idx + 1) / N_HEADS)
    is_b = (idx % 4) == 3
    return slopes[~is_b], slopes[is_b]


def _lane_is_head0(shape):
    return lax.broadcasted_iota(jnp.int32, shape, 1) < HEAD_DIM


def _gather_rows(ref, pieces):
    return jnp.concatenate([ref[0, s:s + n, :] for s, n in pieces], axis=0)


def _dilated_pieces(branch, blk):
    if branch == 0:
        return [(CLASS_ROWS * r + 8 * blk, 8) for r in range(N_CLASSES)]
    if branch == 1:
        r4, b = divmod(blk, 4)
        return [(CLASS_ROWS * (r4 + 4 * c) + 32 * b, 32) for c in range(4)]
    return [(CLASS_ROWS * blk, CLASS_ROWS)]


def _dilated_local_pos():
    u = np.arange(BAND)
    return [16 * (u % 8) + u // 8, 4 * (u % 32) + u // 32, u]


def _dilated_bias_tables():
    slopes_a, _ = _alibi_slopes()
    pos = _dilated_local_pos()
    t12 = np.zeros((N_HEADS_A, 2, BAND, 2 * BAND), np.float64)
    t3 = np.zeros((N_HEADS_A, BAND, BAND), np.float64)
    for br in range(2):
        dil = DILATED_CONFIGS[br][1]
        a_q = pos[br][:, None]
        j_band = np.concatenate([pos[br], BAND + pos[br]])[None, :]
        delta = a_q + BAND - j_band
        valid = (delta >= 0) & (delta <= BAND)
        for h in range(N_HEADS_A):
            t12[h, br] = np.where(valid, -LOG2E * slopes_a[h] * delta * dil, NEG)
    dil = DILATED_CONFIGS[2][1]
    delta = pos[2][:, None] - pos[2][None, :]
    for h in range(N_HEADS_A):
        t3[h] = np.where(delta >= 0, -LOG2E * slopes_a[h] * delta * dil, NEG)
    t12 = t12.reshape(N_HEADS_A // 2, 2, 2, BAND, 2 * BAND).transpose(0, 2, 1, 3, 4)
    return (t12.reshape(N_HEADS_A // 2, 2, 2 * BAND, 2 * BAND).astype(np.float32),
            t3.reshape(N_HEADS_A // 2, 2 * BAND, BAND).astype(np.float32))


def _dilated_kernel(q_ref, k_ref, v_ref, t12_ref, t3_ref, o_ref,
                    q2_ref, ks_ref, vs_ref, acc_ref, m_ref, l_ref):
    n_blocks = SEQ // BAND
    head0 = _lane_is_head0((BAND, LANES))
    ones = jnp.ones((2 * BAND, LANES), BF16)

    def per_head(x):
        return jnp.where(head0, x[:BAND], x[BAND:])

    def stage(pp, br, blk):
        gather = lambda ref: jnp.concatenate([ref[0, pp, s:s + n, :] for s, n in _dilated_pieces(br, blk)], axis=0)
        rows = slice(BAND * blk, BAND * (blk + 1))
        qblk = gather(q_ref) * (LOG2E * HEAD_DIM ** -0.5)
        q2_ref[pp, br, 2 * BAND * blk:2 * BAND * blk + BAND, :] = jnp.where(head0, qblk, 0.0).astype(BF16)
        q2_ref[pp, br, 2 * BAND * blk + BAND:2 * BAND * (blk + 1), :] = jnp.where(head0, 0.0, qblk).astype(BF16)
        ks_ref[pp, br, rows, :] = gather(k_ref).astype(BF16)
        vs_ref[pp, br, rows, :] = gather(v_ref).astype(BF16)

    def block_steps(pp, br, blk):
        first = (blk == 0) if br == 0 else (blk % 4 == 0) if br == 1 else True
        lo = BAND * blk if first else BAND * (blk - 1)
        keys = slice(lo, BAND * (blk + 1))
        n_keys = BAND * (blk + 1) - lo
        val = {}

        def scores():
            table = t3_ref[pp] if br == 2 else t12_ref[pp, br, :, 2 * BAND - n_keys:]
            s2 = _dot_nt(q2_ref[pp, br, 2 * BAND * blk:2 * BAND * (blk + 1), :], ks_ref[pp, br, keys, :]) + table
            val["s"], val["m"] = s2, jnp.max(s2, axis=1, keepdims=True)

        def weigh():
            p = jnp.exp2(val.pop("s") - val["m"]).astype(BF16)
            val["pv"] = _dot(p, jnp.concatenate([vs_ref[pp, br, keys, :], ones[:n_keys]], axis=1))

        def results():
            pv = val.pop("pv")
            acc, l = per_head(pv[:, :LANES]), per_head(pv[:, LANES:])
            m = per_head(jnp.broadcast_to(val.pop("m"), (2 * BAND, LANES)))
            if br < 2:
                off = 0
                for s, n in _dilated_pieces(br, blk):
                    acc_ref[pp, br, s:s + n, :] = acc[off:off + n]
                    m_ref[pp, br, s:s + n, :] = m[off:off + n]
                    l_ref[pp, br, s:s + n, :] = l[off:off + n]
                    off += n
                return
            rows = slice(CLASS_ROWS * blk, CLASS_ROWS * (blk + 1))
            m1, m2 = m_ref[pp, 0, rows, :], m_ref[pp, 1, rows, :]
            m_all = jnp.maximum(jnp.maximum(m1, m2), m)
            w1, w2, w3 = jnp.exp2(m1 - m_all), jnp.exp2(m2 - m_all), jnp.exp2(m - m_all)
            num = w1 * acc_ref[pp, 0, rows, :] + w2 * acc_ref[pp, 1, rows, :] + w3 * acc
            den = w1 * l_ref[pp, 0, rows, :] + w2 * l_ref[pp, 1, rows, :] + w3 * l
            o_ref[0, pp, rows, :] = (num / den).astype(BF16)

        return scores, weigh, results

    def pair_steps(pp):
        blocks = [block_steps(pp, br, blk) for br in range(3) for blk in range(n_blocks)]
        depth = 3

        def iteration(i):
            if i < 2 * n_blocks:
                stage(pp, 1 + i // n_blocks, i % n_blocks)
            for d in range(depth):
                if 0 <= i - d < len(blocks):
                    blocks[i - d][d]()

        first = [lambda blk=blk: stage(pp, 0, blk) for blk in range(n_blocks)]
        return first + [lambda i=i: iteration(i) for i in range(len(blocks) + depth - 1)]

    steps = [pair_steps(pp) for pp in range(PAIRS_PER_STEP)]
    offset = len(steps[0]) // 2
    for i in range(len(steps[0]) + offset * (PAIRS_PER_STEP - 1)):
        for pp in range(PAIRS_PER_STEP):
            if 0 <= i - offset * pp < len(steps[pp]):
                steps[pp][i - offset * pp]()


def _dilated_attention(qkv, t12, t3):
    batch = qkv.shape[0]
    n_steps = N_HEADS_A // 2 // PAIRS_PER_STEP
    group = lambda base: pl.BlockSpec((1, PAIRS_PER_STEP, SEQ, LANES), lambda si, bi: (bi, base + si, 0, 0))
    return pl.pallas_call(
        _dilated_kernel,
        grid=(n_steps, batch),
        in_specs=[
            group(0), group(n_steps), group(2 * n_steps),
            pl.BlockSpec((PAIRS_PER_STEP, 2, 2 * BAND, 2 * BAND), lambda si, bi: (si, 0, 0, 0)),
            pl.BlockSpec((PAIRS_PER_STEP, 2 * BAND, BAND), lambda si, bi: (si, 0, 0)),
        ],
        out_specs=group(0),
        out_shape=jax.ShapeDtypeStruct((batch, N_HEADS_A // 2, SEQ, LANES), BF16),
        scratch_shapes=[
            pltpu.VMEM((PAIRS_PER_STEP, 3, 2 * SEQ, LANES), BF16),
            pltpu.VMEM((PAIRS_PER_STEP, 3, SEQ, LANES), BF16),
            pltpu.VMEM((PAIRS_PER_STEP, 3, SEQ, LANES), BF16),
            pltpu.VMEM((PAIRS_PER_STEP, 2, SEQ, LANES), F32),
            pltpu.VMEM((PAIRS_PER_STEP, 2, SEQ, LANES), F32),
            pltpu.VMEM((PAIRS_PER_STEP, 2, SEQ, LANES), F32),
        ],
        compiler_params=pltpu.CompilerParams(
            dimension_semantics=("parallel", "parallel"), vmem_limit_bytes=VMEM_LIMIT),
        name="dilated_attention",
    )(qkv, qkv, qkv, t12, t3)


N_MOBA_BLOCKS = SEQ // MOBA_BLOCK
MOBA_PIECE = MOBA_BLOCK // N_CLASSES
SEL_LANES = 2 * N_MOBA_BLOCKS
POS_PIECES = 3
V_ROWS = HEAD_DIM + BF16_ROWS


def _moba_pieces(blk):
    return [(CLASS_ROWS * r + MOBA_PIECE * blk, MOBA_PIECE) for r in range(N_CLASSES)]


def _moba_tables():
    _, slopes_b = _alibi_slopes()
    u = np.arange(MOBA_BLOCK)
    pos = N_CLASSES * (u % MOBA_PIECE) + u // MOBA_PIECE
    kaug = np.zeros((N_HEADS_B, SEQ, LANES), np.float32)
    for h in range(N_HEADS_B):
        for blk in range(N_MOBA_BLOCKS):
            rows = slice(MOBA_BLOCK * blk, MOBA_BLOCK * (blk + 1))
            kaug[h, rows, N_MOBA_BLOCKS * (h % 2) + blk] = 1.0
            rest = LOG2E * slopes_b[h] * (MOBA_BLOCK * blk + pos).astype(np.float64)
            for piece in range(POS_PIECES):
                part = rest.astype(BF16).astype(np.float64)
                kaug[h, rows, SEL_LANES + piece] = part
                rest = rest - part
    causal = np.where(pos[None, :] >= pos[:, None], 0.0, NEG).astype(np.float32)
    return kaug.astype(BF16), causal


def _moba_pair_steps(q_ref, k_ref, v_ref, kaug_ref, causal_ref, o_ref,
                     qs_ref, ka_ref, kb_ref, vt_ref, s_refs, p_refs):
    head0 = _lane_is_head0((MOBA_BLOCK, LANES))
    n_sel = min(MOBA_TOPK, N_MOBA_BLOCKS - 1)
    ones_rows = jnp.ones((BF16_ROWS, MOBA_BLOCK), BF16)
    blk_id = lax.broadcasted_iota(jnp.int32, (N_MOBA_BLOCKS, MOBA_BLOCK), 0)
    pos_rows = jnp.where(blk_id < POS_PIECES, 1.0, 0.0)
    pad_rows = jnp.zeros((LANES - SEL_LANES - N_MOBA_BLOCKS, MOBA_BLOCK), F32)
    kmean_h = []

    def stage():
        kmean_rows = []
        for blk in range(N_MOBA_BLOCKS):
            pieces = _moba_pieces(blk)
            rows = slice(MOBA_BLOCK * blk, MOBA_BLOCK * (blk + 1))
            qs_ref[rows, :LANES] = (_gather_rows(q_ref, pieces) * (LOG2E * HEAD_DIM ** -0.5)).astype(BF16)
            kblk = _gather_rows(k_ref, pieces)
            ka_ref[rows, :] = jnp.where(head0, kblk, 0.0).astype(BF16)
            kb_ref[rows, :] = jnp.where(head0, 0.0, kblk).astype(BF16)
            vt = _gather_rows(v_ref, pieces).T
            for h in range(2):
                vt_ref[V_ROWS * h:V_ROWS * h + HEAD_DIM, rows] = vt[HEAD_DIM * h:HEAD_DIM * (h + 1)].astype(BF16)
                vt_ref[V_ROWS * h + HEAD_DIM:V_ROWS * (h + 1), rows] = ones_rows
            kmean_rows.append(jnp.sum(kblk, axis=0, keepdims=True) * (1.0 / MOBA_BLOCK))
        kmean = jnp.concatenate(kmean_rows, axis=0)
        head0_k = _lane_is_head0((N_MOBA_BLOCKS, LANES))
        kmean_h.extend([jnp.where(head0_k, kmean, 0.0), jnp.where(head0_k, 0.0, kmean)])

    def query_extra_lanes(n):
        sel = [jnp.zeros((N_MOBA_BLOCKS, MOBA_BLOCK), F32)] * 2
        if n > n_sel:
            q_f32 = _gather_rows(q_ref, _moba_pieces(n))
            for h in range(2):
                gate = _dot_nt(kmean_h[h], q_f32, precision=lax.Precision.HIGHEST)
                for m in range(n):
                    g_m = gate[m:m + 1, :]
                    beats = ((gate > g_m) | ((gate == g_m) & (blk_id < m))) & (blk_id < n)
                    rank = jnp.sum(beats.astype(F32), axis=0, keepdims=True)
                    sel[h] = jnp.where((blk_id == m) & (rank >= n_sel), NEG, sel[h])
        q_extra = jnp.concatenate([sel[0], sel[1], pos_rows, pad_rows], axis=0).T
        qs_ref[MOBA_BLOCK * n:MOBA_BLOCK * (n + 1), LANES:] = q_extra.astype(BF16)

    def score_steps(n, slot, fold):
        def step(m):
            k_rows = slice(MOBA_BLOCK * m, MOBA_BLOCK * (m + 1))
            k_aug = jnp.concatenate(
                [jnp.concatenate([ka_ref[k_rows, :], kaug_ref[0, k_rows, :]], axis=1),
                 jnp.concatenate([kb_ref[k_rows, :], kaug_ref[1, k_rows, :]], axis=1)], axis=0)
            s2 = _dot_nt(k_aug, qs_ref[MOBA_BLOCK * n:MOBA_BLOCK * (n + 1), :])
            for h in range(2):
                s = s2[h * MOBA_BLOCK:(h + 1) * MOBA_BLOCK]
                if m == n:
                    s = s + causal_ref[...]
                s_refs[slot][h, k_rows, :] = s
                parts = [s[SUBLANES * i:SUBLANES * (i + 1)] for i in range(MOBA_BLOCK // SUBLANES)]
                if fold[h] is not None:
                    parts.append(fold[h])
                while len(parts) > 1:
                    parts = [jnp.maximum(a, b) for a, b in zip(parts[::2], parts[1::2])] + parts[len(parts) & ~1:]
                fold[h] = parts[0]

        return [lambda m=m: step(m) for m in range(n + 1)]

    def output_steps(n, slot, fold):
        n_keys = MOBA_BLOCK * (n + 1)
        mx = []

        def probs(m):
            if not mx:
                mx.extend(jnp.max(fold[h], axis=0, keepdims=True) for h in range(2))
            k_rows = slice(MOBA_BLOCK * m, MOBA_BLOCK * (m + 1))
            for h in range(2):
                p_refs[slot][h, k_rows, :] = jnp.exp2(s_refs[slot][h, k_rows, :] - mx[h]).astype(BF16)

        def finish():
            acc = [_dot(vt_ref[V_ROWS * h:V_ROWS * (h + 1), :n_keys], p_refs[slot][h, :n_keys, :])
                   for h in range(2)]
            out_t = jnp.concatenate(
                [acc[h][:HEAD_DIM] / acc[h][HEAD_DIM:HEAD_DIM + 1] for h in range(2)], axis=0)
            out = out_t.T.astype(BF16)
            off = 0
            for s, cnt in _moba_pieces(n):
                o_ref[0, s:s + cnt, :] = out[off:off + cnt]
                off += cnt

        return [lambda m=m: probs(m) for m in range(n + 1)] + [finish]

    order = list(range(N_MOBA_BLOCKS))
    late = [n for n in order if n > n_sel]
    emit = [stage] + [lambda n=n: query_extra_lanes(n) for n in order if n <= n_sel]
    folds = [[None, None] for _ in range(N_MOBA_BLOCKS)]
    for pos in range(N_MOBA_BLOCKS + 1):
        first = score_steps(order[pos], pos % 2, folds[order[pos]]) if pos < N_MOBA_BLOCKS else []
        second = output_steps(order[pos - 1], (pos - 1) % 2, folds[order[pos - 1]]) if pos > 0 else []
        third = [lambda n=n: query_extra_lanes(n) for n in late[2 * pos:2 * pos + 2]]
        for i in range(max(len(first), len(second), len(third))):
            emit.extend(steps[i] for steps in (first, second, third) if i < len(steps))
    return emit


def _moba_kernel(q_ref, k_ref, v_ref, kaug_ref, causal_ref, o_ref,
                 qs_ref, ka_ref, kb_ref, vt_ref, s0_ref, s1_ref, p0_ref, p1_ref):
    steps = []
    for pp in range(N_HEADS_B // 2):
        steps.append(_moba_pair_steps(
            q_ref.at[:, pp], k_ref.at[:, pp], v_ref.at[:, pp],
            kaug_ref.at[2 * pp:2 * pp + 2], causal_ref, o_ref.at[:, pp],
            qs_ref.at[pp], ka_ref.at[pp], kb_ref.at[pp], vt_ref.at[pp],
            (s0_ref.at[pp], s1_ref.at[pp]), (p0_ref.at[pp], p1_ref.at[pp])))
    offset = len(steps[0]) // 2
    for i in range(len(steps[0]) + offset):
        if i < len(steps[0]):
            steps[0][i]()
        if 0 <= i - offset < len(steps[1]):
            steps[1][i - offset]()


def _moba_attention(qkv, kaug, causal):
    batch = qkv.shape[0]
    n_pairs = N_HEADS_B // 2
    group = lambda idx: pl.BlockSpec((1, n_pairs, SEQ, LANES), lambda bi: (bi, idx, 0, 0))
    base_q = 3 * WIDTH_A // WIDTH_B
    return pl.pallas_call(
        _moba_kernel,
        grid=(batch,),
        in_specs=[
            group(base_q), group(base_q + 1), group(base_q + 2),
            pl.BlockSpec((N_HEADS_B, SEQ, LANES), lambda bi: (0, 0, 0)),
            pl.BlockSpec((MOBA_BLOCK, MOBA_BLOCK), lambda bi: (0, 0)),
        ],
        out_specs=group(0),
        out_shape=jax.ShapeDtypeStruct((batch, n_pairs, SEQ, LANES), BF16),
        scratch_shapes=[
            pltpu.VMEM((n_pairs, SEQ, 2 * LANES), BF16),
            pltpu.VMEM((n_pairs, SEQ, LANES), BF16),
            pltpu.VMEM((n_pairs, SEQ, LANES), BF16),
            pltpu.VMEM((n_pairs, 2 * V_ROWS, SEQ), BF16),
            pltpu.VMEM((n_pairs, 2, SEQ, MOBA_BLOCK), F32),
            pltpu.VMEM((n_pairs, 2, SEQ, MOBA_BLOCK), F32),
            pltpu.VMEM((n_pairs, 2, SEQ, MOBA_BLOCK), BF16),
            pltpu.VMEM((n_pairs, 2, SEQ, MOBA_BLOCK), BF16),
        ],
        compiler_params=pltpu.CompilerParams(
            dimension_semantics=("parallel",), vmem_limit_bytes=VMEM_LIMIT),
        name="moba_attention",
    )(qkv, qkv, qkv, kaug, causal)


def kernel(x, ffn1_gate, ffn1_up, ffn1_down, ln1_g, ln1_b, w_in, w_out, ln2_g, ln2_b,
           ffn2_gate, ffn2_up, ffn2_down, ln3_g, ln3_b):
    assert x.shape[1:] == (SEQ, D_MODEL) and ffn1_gate.shape == (DEPTH, D_MODEL, D_FF)
    bf = lambda w: w[0].astype(BF16)
    row = lambda p: p[0].reshape(1, D_MODEL).astype(F32)
    t12, t3 = _dilated_bias_tables()
    kaug, causal = _moba_tables()
    perm = _tile_permutation()

    h1, qkv = _ffn_qkv(x, bf(ffn1_gate), bf(ffn1_up), bf(ffn1_down), row(ln1_g), row(ln1_b), bf(w_in),
                       jnp.asarray(perm, BF16))
    oa = _dilated_attention(qkv, jnp.asarray(t12), jnp.asarray(t3))
    ob = _moba_attention(qkv, jnp.asarray(kaug), jnp.asarray(causal))
    w_o = bf(w_out)
    return _out_ffn(oa, ob, h1, jnp.asarray(perm.T, BF16), w_o[:WIDTH_A], w_o[WIDTH_A:], row(ln2_g), row(ln2_b),
                    bf(ffn2_gate), bf(ffn2_up), bf(ffn2_down), row(ln3_g), row(ln3_b))
```

```python
import numpy as np
import jax
import jax.numpy as jnp
from jax import lax
from jax.experimental import pallas as pl
from jax.experimental.pallas import tpu as pltpu

D_MODEL = 1024
SEQ = 2048
DEPTH = 1
HEAD_DIM = 64
N_HEADS = 16
N_HEADS_B = 4
N_HEADS_A = 12
WIDTH_A = N_HEADS_A * HEAD_DIM
WIDTH_B = N_HEADS_B * HEAD_DIM
MIX_WIDTH = WIDTH_A + WIDTH_B
DILATED_CONFIGS = ((128, 1), (512, 4), (2048, 16))
MOBA_BLOCK = 256
MOBA_TOPK = 3
D_FF = 2816
LN_EPS = 1e-5
ALPHA = (2.0 * DEPTH) ** 0.25

LANES = 128
QKV_GROUPS = 3 * MIX_WIDTH // LANES
SUBLANES = 8
BF16_ROWS = 16
N_CLASSES = 16
CLASS_ROWS = SEQ // N_CLASSES
BAND = 128
PAIRS_PER_STEP = 2
ROW_TILE = 512
OUT_ROW_TILE = 1024
TILE_CLASS_ROWS = ROW_TILE // N_CLASSES
SUB_TILE = 256
SUB_CLASS_ROWS = SUB_TILE // N_CLASSES
FF_CHUNK = 256
NEG = -1e30
LOG2E = 1.4426950408889634
VMEM_LIMIT = 56 * 1024 * 1024

F32 = jnp.float32
BF16 = jnp.bfloat16


def _dot(a, b):
    return jnp.dot(a, b, preferred_element_type=F32)


def _dot_nt(a, b, precision=None):
    return lax.dot_general(a, b, (((1,), (1,)), ((), ())), precision=precision,
                           preferred_element_type=F32)


def _layer_norm(z, g, b):
    mu = jnp.mean(z, axis=-1, keepdims=True)
    zc = z - mu
    var = jnp.mean(zc * zc, axis=-1, keepdims=True)
    return zc * lax.rsqrt(var + LN_EPS) * g + b


def _swiglu_steps(get_xb, wg_ref, wu_ref, act_ref):
    def chunk(c):
        cols = slice(c * FF_CHUNK, (c + 1) * FF_CHUNK)
        xb = get_xb()
        g = _dot(xb, wg_ref[:, cols])
        u = _dot(xb, wu_ref[:, cols])
        act_ref[:, cols] = (g / (1.0 + jnp.exp(-g)) * u).astype(BF16)

    return [lambda c=c: chunk(c) for c in range(D_FF // FF_CHUNK)]


def _emit_spread(main_steps, side_steps):
    gap = max(1, len(main_steps) // (len(side_steps) + 1))
    side = list(side_steps)
    for i, step in enumerate(main_steps):
        step()
        if side and (i + 1) % gap == 0:
            side.pop(0)()
    for step in side:
        step()


def _tile_permutation():
    u = np.arange(SUB_TILE)
    perm = np.zeros((SUB_TILE, SUB_TILE), np.float32)
    perm[u, N_CLASSES * (u % SUB_CLASS_ROWS) + u // SUB_CLASS_ROWS] = 1.0
    return perm


def _ffn_qkv_kernel(x_ref, wg_ref, wu_ref, wd_ref, g_ref, b_ref, win_ref, perm_ref,
                    h_ref, qkv_ref, *act_refs):
    def sub_tile(sub):
        rows = slice(sub * SUB_TILE, (sub + 1) * SUB_TILE)
        act_ref = act_refs[sub]
        val = {}

        def norm():
            x = x_ref[0, rows, :]
            y = _dot(act_ref[...], wd_ref[...])
            h = _layer_norm(ALPHA * x + 0.5 * y, g_ref[...], b_ref[...])
            h_ref[0, rows, :] = h
            val["hb"] = h.astype(BF16)

        def reorder():
            val["hb"] = _dot(perm_ref[...], val["hb"]).astype(BF16)

        def project():
            qkv = _dot(val.pop("hb"), win_ref[...])
            for c in range(QKV_GROUPS):
                for r in range(N_CLASSES):
                    qkv_ref[0, c, r, sub * SUB_CLASS_ROWS:(sub + 1) * SUB_CLASS_ROWS, :] = (
                        qkv[r * SUB_CLASS_ROWS:(r + 1) * SUB_CLASS_ROWS, c * LANES:(c + 1) * LANES])

        chunks = _swiglu_steps(lambda: x_ref[0, rows, :].astype(BF16), wg_ref, wu_ref, act_ref)
        return chunks, [norm, reorder, project]

    chunks0, (norm0, reorder0, project0) = sub_tile(0)
    chunks1, (norm1, reorder1, project1) = sub_tile(1)
    for step in chunks0:
        step()
    _emit_spread(chunks1, [norm0, reorder0])
    norm1()
    project0()
    reorder1()
    project1()


def _const_spec(shape):
    return pl.BlockSpec(shape, lambda *_: (0,) * len(shape), pipeline_mode=pl.Buffered(1))


def _ffn_qkv(x, wg, wu, wd, g, b, w_in, perm):
    batch = x.shape[0]
    n_tiles = SEQ // ROW_TILE
    h1, qkv = pl.pallas_call(
        _ffn_qkv_kernel,
        grid=(batch, n_tiles),
        in_specs=[
            pl.BlockSpec((1, ROW_TILE, D_MODEL), lambda bi, ti: (bi, ti, 0)),
            _const_spec((D_MODEL, D_FF)), _const_spec((D_MODEL, D_FF)), _const_spec((D_FF, D_MODEL)),
            _const_spec((1, D_MODEL)), _const_spec((1, D_MODEL)),
            _const_spec((D_MODEL, 3 * MIX_WIDTH)), _const_spec((SUB_TILE, SUB_TILE)),
        ],
        out_specs=[
            pl.BlockSpec((1, ROW_TILE, D_MODEL), lambda bi, ti: (bi, ti, 0)),
            pl.BlockSpec((1, QKV_GROUPS, N_CLASSES, TILE_CLASS_ROWS, LANES), lambda bi, ti: (bi, 0, 0, ti, 0)),
        ],
        out_shape=[
            jax.ShapeDtypeStruct((batch, SEQ, D_MODEL), F32),
            jax.ShapeDtypeStruct((batch, QKV_GROUPS, N_CLASSES, CLASS_ROWS, LANES), F32),
        ],
        scratch_shapes=[pltpu.VMEM((SUB_TILE, D_FF), BF16)] * (ROW_TILE // SUB_TILE),
        compiler_params=pltpu.CompilerParams(
            dimension_semantics=("parallel", "parallel"), vmem_limit_bytes=VMEM_LIMIT),
        name="ffn1_ln1_qkv",
    )(x, wg, wu, wd, g, b, w_in, perm)
    return h1, qkv.reshape(batch, QKV_GROUPS, SEQ, LANES)


def _out_ffn_kernel(oa_ref, ob_ref, h_ref, unperm_ref, woa_ref, wob_ref, g2_ref, b2_ref,
                    wg_ref, wu_ref, wd_ref, g3_ref, b3_ref, out_ref, *act_refs):
    def natural_rows(o_ref, sub):
        part = slice(sub * SUB_CLASS_ROWS, (sub + 1) * SUB_CLASS_ROWS)
        o_res = jnp.concatenate(
            [jnp.concatenate([o_ref[0, p, r, part, :] for p in range(o_ref.shape[1])], axis=1)
             for r in range(N_CLASSES)], axis=0)
        return _dot(unperm_ref[...], o_res).astype(BF16)

    def sub_tile(sub):
        rows = slice(sub * SUB_TILE, (sub + 1) * SUB_TILE)
        act_ref = act_refs[sub]
        val = {}

        def mix():
            val["mix"] = (_dot(natural_rows(oa_ref, sub), woa_ref[...])
                          + _dot(natural_rows(ob_ref, sub), wob_ref[...]))

        def norm():
            val["h2"] = _layer_norm(ALPHA * h_ref[0, rows, :] + val.pop("mix"), g2_ref[...], b2_ref[...])
            val["h2b"] = val["h2"].astype(BF16)

        def ffn_norm():
            y = _dot(act_ref[...], wd_ref[...])
            out_ref[0, rows, :] = _layer_norm(ALPHA * val.pop("h2") + 0.5 * y, g3_ref[...], b3_ref[...])

        chunks = _swiglu_steps(lambda: val["h2b"], wg_ref, wu_ref, act_ref)
        return mix, norm, chunks, ffn_norm

    subs = [sub_tile(sub) for sub in range(len(act_refs))]
    mix, norm, chunks, tail = zip(*subs)
    mix[0]()
    mix[1]()
    norm[0]()
    for sub in range(len(subs)):
        side = [tail[sub - 1]] if sub > 0 else []
        side += [norm[sub + 1]] if sub + 1 < len(subs) else []
        side += [mix[sub + 2]] if sub + 2 < len(subs) else []
        _emit_spread(chunks[sub], side)
    tail[-1]()


def _out_ffn(oa, ob, h1, unperm, woa, wob, g2, b2, wg, wu, wd, g3, b3):
    batch = h1.shape[0]
    n_tiles = SEQ // OUT_ROW_TILE
    res_spec = lambda width: pl.BlockSpec((1, width // LANES, N_CLASSES, OUT_ROW_TILE // N_CLASSES, LANES),
                                          lambda bi, ti: (bi, 0, 0, ti, 0))
    return pl.pallas_call(
        _out_ffn_kernel,
        grid=(batch, n_tiles),
        in_specs=[
            res_spec(WIDTH_A), res_spec(WIDTH_B),
            pl.BlockSpec((1, OUT_ROW_TILE, D_MODEL), lambda bi, ti: (bi, ti, 0)),
            _const_spec((SUB_TILE, SUB_TILE)),
            _const_spec((WIDTH_A, D_MODEL)), _const_spec((WIDTH_B, D_MODEL)),
            _const_spec((1, D_MODEL)), _const_spec((1, D_MODEL)),
            _const_spec((D_MODEL, D_FF)), _const_spec((D_MODEL, D_FF)), _const_spec((D_FF, D_MODEL)),
            _const_spec((1, D_MODEL)), _const_spec((1, D_MODEL)),
        ],
        out_specs=pl.BlockSpec((1, OUT_ROW_TILE, D_MODEL), lambda bi, ti: (bi, ti, 0)),
        out_shape=jax.ShapeDtypeStruct((batch, SEQ, D_MODEL), F32),
        scratch_shapes=[pltpu.VMEM((SUB_TILE, D_FF), BF16)] * (OUT_ROW_TILE // SUB_TILE),
        compiler_params=pltpu.CompilerParams(
            dimension_semantics=("parallel", "parallel"), vmem_limit_bytes=VMEM_LIMIT),
        name="outproj_ln2_ffn2_ln3",
    )(oa.reshape(batch, WIDTH_A // LANES, N_CLASSES, CLASS_ROWS, LANES),
      ob.reshape(batch, WIDTH_B // LANES, N_CLASSES, CLASS_ROWS, LANES),
      h1, unperm, woa, wob, g2, b2, wg, wu, wd, g3, b3)


def _alibi_slopes():
    idx = np.arange(N_HEADS)
    slopes = 2.0 ** (-8.0 * (idx + 1) / N_HEADS)
    is_b = (idx % 4) == 3
    return slopes[~is_b], slopes[is_b]


def _lane_is_head0(shape):
    return lax.broadcasted_iota(jnp.int32, shape, 1) < HEAD_DIM


def _gather_rows(ref, pieces):
    return jnp.concatenate([ref[0, s:s + n, :] for s, n in pieces], axis=0)


def _dilated_pieces(branch, blk):
    if branch == 0:
        return [(CLASS_ROWS * r + 8 * blk, 8) for r in range(N_CLASSES)]
    if branch == 1:
        r4, b = divmod(blk, 4)
        return [(CLASS_ROWS * (r4 + 4 * c) + 32 * b, 32) for c in range(4)]
    return [(CLASS_ROWS * blk, CLASS_ROWS)]


def _dilated_local_pos():
    u = np.arange(BAND)
    return [16 * (u % 8) + u // 8, 4 * (u % 32) + u // 32, u]


def _dilated_bias_tables():
    slopes_a, _ = _alibi_slopes()
    pos = _dilated_local_pos()
    t12 = np.zeros((N_HEADS_A, 2, BAND, 2 * BAND), np.float64)
    t3 = np.zeros((N_HEADS_A, BAND, BAND), np.float64)
    for br in range(2):
        dil = DILATED_CONFIGS[br][1]
        a_q = pos[br][:, None]
        j_band = np.concatenate([pos[br], BAND + pos[br]])[None, :]
        delta = a_q + BAND - j_band
        valid = (delta >= 0) & (delta <= BAND)
        for h in range(N_HEADS_A):
            t12[h, br] = np.where(valid, -LOG2E * slopes_a[h] * delta * dil, NEG)
    dil = DILATED_CONFIGS[2][1]
    delta = pos[2][:, None] - pos[2][None, :]
    for h in range(N_HEADS_A):
        t3[h] = np.where(delta >= 0, -LOG2E * slopes_a[h] * delta * dil, NEG)
    t12 = t12.reshape(N_HEADS_A // 2, 2, 2, BAND, 2 * BAND).transpose(0, 2, 1, 3, 4)
    return (t12.reshape(N_HEADS_A // 2, 2, 2 * BAND, 2 * BAND).astype(np.float32),
            t3.reshape(N_HEADS_A // 2, 2 * BAND, BAND).astype(np.float32))


def _dilated_kernel(q_ref, k_ref, v_ref, t12_ref, t3_ref, o_ref,
                    q2_ref, ks_ref, vs_ref, acc_ref, m_ref, l_ref):
    n_blocks = SEQ // BAND
    head0 = _lane_is_head0((BAND, LANES))
    ones = jnp.ones((2 * BAND, LANES), BF16)

    def per_head(x):
        return jnp.where(head0, x[:BAND], x[BAND:])

    def stage(pp, br, blk):
        gather = lambda ref: jnp.concatenate([ref[0, pp, s:s + n, :] for s, n in _dilated_pieces(br, blk)], axis=0)
        rows = slice(BAND * blk, BAND * (blk + 1))
        qblk = gather(q_ref) * (LOG2E * HEAD_DIM ** -0.5)
        q2_ref[pp, br, 2 * BAND * blk:2 * BAND * blk + BAND, :] = jnp.where(head0, qblk, 0.0).astype(BF16)
        q2_ref[pp, br, 2 * BAND * blk + BAND:2 * BAND * (blk + 1), :] = jnp.where(head0, 0.0, qblk).astype(BF16)
        ks_ref[pp, br, rows, :] = gather(k_ref).astype(BF16)
        vs_ref[pp, br, rows, :] = gather(v_ref).astype(BF16)

    def block_steps(pp, br, blk):
        first = (blk == 0) if br == 0 else (blk % 4 == 0) if br == 1 else True
        lo = BAND * blk if first else BAND * (blk - 1)
        keys = slice(lo, BAND * (blk + 1))
        n_keys = BAND * (blk + 1) - lo
        val = {}

        def scores():
            table = t3_ref[pp] if br == 2 else t12_ref[pp, br, :, 2 * BAND - n_keys:]
            s2 = _dot_nt(q2_ref[pp, br, 2 * BAND * blk:2 * BAND * (blk + 1), :], ks_ref[pp, br, keys, :]) + table
            val["s"], val["m"] = s2, jnp.max(s2, axis=1, keepdims=True)

        def weigh():
            p = jnp.exp2(val.pop("s") - val["m"]).astype(BF16)
            val["pv"] = _dot(p, jnp.concatenate([vs_ref[pp, br, keys, :], ones[:n_keys]], axis=1))

        def results():
            pv = val.pop("pv")
            acc, l = per_head(pv[:, :LANES]), per_head(pv[:, LANES:])
            m = per_head(jnp.broadcast_to(val.pop("m"), (2 * BAND, LANES)))
            if br < 2:
                off = 0
                for s, n in _dilated_pieces(br, blk):
                    acc_ref[pp, br, s:s + n, :] = acc[off:off + n]
                    m_ref[pp, br, s:s + n, :] = m[off:off + n]
                    l_ref[pp, br, s:s + n, :] = l[off:off + n]
                    off += n
                return
            rows = slice(CLASS_ROWS * blk, CLASS_ROWS * (blk + 1))
            m1, m2 = m_ref[pp, 0, rows, :], m_ref[pp, 1, rows, :]
            m_all = jnp.maximum(jnp.maximum(m1, m2), m)
            w1, w2, w3 = jnp.exp2(m1 - m_all), jnp.exp2(m2 - m_all), jnp.exp2(m - m_all)
            num = w1 * acc_ref[pp, 0, rows, :] + w2 * acc_ref[pp, 1, rows, :] + w3 * acc
            den = w1 * l_ref[pp, 0, rows, :] + w2 * l_ref[pp, 1, rows, :] + w3 * l
            o_ref[0, pp, rows, :] = (num / den).astype(BF16)

        return scores, weigh, results

    def pair_steps(pp):
        blocks = [block_steps(pp, br, blk) for br in range(3) for blk in range(n_blocks)]
        depth = 3

        def iteration(i):
            if i < 2 * n_blocks:
                stage(pp, 1 + i // n_blocks, i % n_blocks)
            for d in range(depth):
                if 0 <= i - d < len(blocks):
                    blocks[i - d][d]()

        first = [lambda blk=blk: stage(pp, 0, blk) for blk in range(n_blocks)]
        return first + [lambda i=i: iteration(i) for i in range(len(blocks) + depth - 1)]

    steps = [pair_steps(pp) for pp in range(PAIRS_PER_STEP)]
    offset = 1
    for i in range(len(steps[0]) + offset * (PAIRS_PER_STEP - 1)):
        for pp in range(PAIRS_PER_STEP):
            if 0 <= i - offset * pp < len(steps[pp]):
                steps[pp][i - offset * pp]()


def _dilated_attention(qkv, t12, t3):
    batch = qkv.shape[0]
    n_steps = N_HEADS_A // 2 // PAIRS_PER_STEP
    group = lambda base: pl.BlockSpec((1, PAIRS_PER_STEP, SEQ, LANES), lambda si, bi: (bi, base + si, 0, 0))
    return pl.pallas_call(
        _dilated_kernel,
        grid=(n_steps, batch),
        in_specs=[
            group(0), group(n_steps), group(2 * n_steps),
            pl.BlockSpec((PAIRS_PER_STEP, 2, 2 * BAND, 2 * BAND), lambda si, bi: (si, 0, 0, 0)),
            pl.BlockSpec((PAIRS_PER_STEP, 2 * BAND, BAND), lambda si, bi: (si, 0, 0)),
        ],
        out_specs=group(0),
        out_shape=jax.ShapeDtypeStruct((batch, N_HEADS_A // 2, SEQ, LANES), BF16),
        scratch_shapes=[
            pltpu.VMEM((PAIRS_PER_STEP, 3, 2 * SEQ, LANES), BF16),
            pltpu.VMEM((PAIRS_PER_STEP, 3, SEQ, LANES), BF16),
            pltpu.VMEM((PAIRS_PER_STEP, 3, SEQ, LANES), BF16),
            pltpu.VMEM((PAIRS_PER_STEP, 2, SEQ, LANES), F32),
            pltpu.VMEM((PAIRS_PER_STEP, 2, SEQ, LANES), F32),
            pltpu.VMEM((PAIRS_PER_STEP, 2, SEQ, LANES), F32),
        ],
        compiler_params=pltpu.CompilerParams(
            dimension_semantics=("parallel", "parallel"), vmem_limit_bytes=VMEM_LIMIT),
        name="dilated_attention",
    )(qkv, qkv, qkv, t12, t3)


N_MOBA_BLOCKS = SEQ // MOBA_BLOCK
MOBA_PIECE = MOBA_BLOCK // N_CLASSES
SEL_LANES = 2 * N_MOBA_BLOCKS
POS_PIECES = 3
V_ROWS = HEAD_DIM + BF16_ROWS


def _moba_pieces(blk):
    return [(CLASS_ROWS * r + MOBA_PIECE * blk, MOBA_PIECE) for r in range(N_CLASSES)]


def _moba_tables():
    _, slopes_b = _alibi_slopes()
    u = np.arange(MOBA_BLOCK)
    pos = N_CLASSES * (u % MOBA_PIECE) + u // MOBA_PIECE
    kaug = np.zeros((N_HEADS_B, SEQ, LANES), np.float32)
    for h in range(N_HEADS_B):
        for blk in range(N_MOBA_BLOCKS):
            rows = slice(MOBA_BLOCK * blk, MOBA_BLOCK * (blk + 1))
            kaug[h, rows, N_MOBA_BLOCKS * (h % 2) + blk] = 1.0
            rest = LOG2E * slopes_b[h] * (MOBA_BLOCK * blk + pos).astype(np.float64)
            for piece in range(POS_PIECES):
                part = rest.astype(BF16).astype(np.float64)
                kaug[h, rows, SEL_LANES + piece] = part
                rest = rest - part
    causal = np.where(pos[None, :] >= pos[:, None], 0.0, NEG).astype(np.float32)
    return kaug.astype(BF16), causal


def _moba_pair_steps(q_ref, k_ref, v_ref, kaug_ref, causal_ref, o_ref,
                     qs_ref, ka_ref, kb_ref, vt_ref, s_refs, p_refs):
    head0 = _lane_is_head0((MOBA_BLOCK, LANES))
    n_sel = min(MOBA_TOPK, N_MOBA_BLOCKS - 1)
    ones_rows = jnp.ones((BF16_ROWS, MOBA_BLOCK), BF16)
    blk_id = lax.broadcasted_iota(jnp.int32, (N_MOBA_BLOCKS, MOBA_BLOCK), 0)
    pos_rows = jnp.where(blk_id < POS_PIECES, 1.0, 0.0)
    pad_rows = jnp.zeros((LANES - SEL_LANES - N_MOBA_BLOCKS, MOBA_BLOCK), F32)
    kmean_h = []

    def stage():
        kmean_rows = []
        for blk in range(N_MOBA_BLOCKS):
            pieces = _moba_pieces(blk)
            rows = slice(MOBA_BLOCK * blk, MOBA_BLOCK * (blk + 1))
            qs_ref[rows, :LANES] = (_gather_rows(q_ref, pieces) * (LOG2E * HEAD_DIM ** -0.5)).astype(BF16)
            kblk = _gather_rows(k_ref, pieces)
            ka_ref[rows, :] = jnp.where(head0, kblk, 0.0).astype(BF16)
            kb_ref[rows, :] = jnp.where(head0, 0.0, kblk).astype(BF16)
            vt = _gather_rows(v_ref, pieces).T
            for h in range(2):
                vt_ref[V_ROWS * h:V_ROWS * h + HEAD_DIM, rows] = vt[HEAD_DIM * h:HEAD_DIM * (h + 1)].astype(BF16)
                vt_ref[V_ROWS * h + HEAD_DIM:V_ROWS * (h + 1), rows] = ones_rows
            kmean_rows.append(jnp.sum(kblk, axis=0, keepdims=True) * (1.0 / MOBA_BLOCK))
        kmean = jnp.concatenate(kmean_rows, axis=0)
        head0_k = _lane_is_head0((N_MOBA_BLOCKS, LANES))
        kmean_h.extend([jnp.where(head0_k, kmean, 0.0), jnp.where(head0_k, 0.0, kmean)])

    def query_extra_lanes(n):
        sel = [jnp.zeros((N_MOBA_BLOCKS, MOBA_BLOCK), F32)] * 2
        if n > n_sel:
            q_f32 = _gather_rows(q_ref, _moba_pieces(n))
            for h in range(2):
                gate = _dot_nt(kmean_h[h], q_f32, precision=lax.Precision.HIGHEST)
                for m in range(n):
                    g_m = gate[m:m + 1, :]
                    beats = ((gate > g_m) | ((gate == g_m) & (blk_id < m))) & (blk_id < n)
                    rank = jnp.sum(beats.astype(F32), axis=0, keepdims=True)
                    sel[h] = jnp.where((blk_id == m) & (rank >= n_sel), NEG, sel[h])
        q_extra = jnp.concatenate([sel[0], sel[1], pos_rows, pad_rows], axis=0).T
        qs_ref[MOBA_BLOCK * n:MOBA_BLOCK * (n + 1), LANES:] = q_extra.astype(BF16)

    def score_steps(n, slot, fold):
        def step(m):
            k_rows = slice(MOBA_BLOCK * m, MOBA_BLOCK * (m + 1))
            k_aug = jnp.concatenate(
                [jnp.concatenate([ka_ref[k_rows, :], kaug_ref[0, k_rows, :]], axis=1),
                 jnp.concatenate([kb_ref[k_rows, :], kaug_ref[1, k_rows, :]], axis=1)], axis=0)
            s2 = _dot_nt(k_aug, qs_ref[MOBA_BLOCK * n:MOBA_BLOCK * (n + 1), :])
            for h in range(2):
                s = s2[h * MOBA_BLOCK:(h + 1) * MOBA_BLOCK]
                if m == n:
                    s = s + causal_ref[...]
                s_refs[slot][h, k_rows, :] = s
                parts = [s[SUBLANES * i:SUBLANES * (i + 1)] for i in range(MOBA_BLOCK // SUBLANES)]
                if fold[h] is not None:
                    parts.append(fold[h])
                while len(parts) > 1:
                    parts = [jnp.maximum(a, b) for a, b in zip(parts[::2], parts[1::2])] + parts[len(parts) & ~1:]
                fold[h] = parts[0]

        return [lambda m=m: step(m) for m in range(n + 1)]

    def output_steps(n, slot, fold):
        n_keys = MOBA_BLOCK * (n + 1)
        mx = []

        def probs(m):
            if not mx:
                mx.extend(jnp.max(fold[h], axis=0, keepdims=True) for h in range(2))
            k_rows = slice(MOBA_BLOCK * m, MOBA_BLOCK * (m + 1))
            for h in range(2):
                p_refs[slot][h, k_rows, :] = jnp.exp2(s_refs[slot][h, k_rows, :] - mx[h]).astype(BF16)

        def finish():
            acc = [_dot(vt_ref[V_ROWS * h:V_ROWS * (h + 1), :n_keys], p_refs[slot][h, :n_keys, :])
                   for h in range(2)]
            out_t = jnp.concatenate(
                [acc[h][:HEAD_DIM] / acc[h][HEAD_DIM:HEAD_DIM + 1] for h in range(2)], axis=0)
            out = out_t.T.astype(BF16)
            off = 0
            for s, cnt in _moba_pieces(n):
                o_ref[0, s:s + cnt, :] = out[off:off + cnt]
                off += cnt

        return [lambda m=m: probs(m) for m in range(n + 1)] + [finish]

    order = list(range(N_MOBA_BLOCKS))
    late = [n for n in order if n > n_sel]
    emit = [stage] + [lambda n=n: query_extra_lanes(n) for n in order if n <= n_sel]
    folds = [[None, None] for _ in range(N_MOBA_BLOCKS)]
    for pos in range(N_MOBA_BLOCKS + 1):
        first = score_steps(order[pos], pos % 2, folds[order[pos]]) if pos < N_MOBA_BLOCKS else []
        second = output_steps(order[pos - 1], (pos - 1) % 2, folds[order[pos - 1]]) if pos > 0 else []
        third = [lambda n=n: query_extra_lanes(n) for n in late[2 * pos:2 * pos + 2]]
        for i in range(max(len(first), len(second), len(third))):
            emit.extend(steps[i] for steps in (first, second, third) if i < len(steps))
    return emit


def _moba_kernel(q_ref, k_ref, v_ref, kaug_ref, causal_ref, o_ref,
                 qs_ref, ka_ref, kb_ref, vt_ref, s0_ref, s1_ref, p0_ref, p1_ref):
    steps = []
    for pp in range(N_HEADS_B // 2):
        steps.append(_moba_pair_steps(
            q_ref.at[:, pp], k_ref.at[:, pp], v_ref.at[:, pp],
            kaug_ref.at[2 * pp:2 * pp + 2], causal_ref, o_ref.at[:, pp],
            qs_ref.at[pp], ka_ref.at[pp], kb_ref.at[pp], vt_ref.at[pp],
            (s0_ref.at[pp], s1_ref.at[pp]), (p0_ref.at[pp], p1_ref.at[pp])))
    offset = len(steps[0]) // 2
    for i in range(len(steps[0]) + offset):
        if i < len(steps[0]):
            steps[0][i]()
        if 0 <= i - offset < len(steps[1]):
            steps[1][i - offset]()


def _moba_attention(qkv, kaug, causal):
    batch = qkv.shape[0]
    n_pairs = N_HEADS_B // 2
    group = lambda idx: pl.BlockSpec((1, n_pairs, SEQ, LANES), lambda bi: (bi, idx, 0, 0))
    base_q = 3 * WIDTH_A // WIDTH_B
    return pl.pallas_call(
        _moba_kernel,
        grid=(batch,),
        in_specs=[
            group(base_q), group(base_q + 1), group(base_q + 2),
            pl.BlockSpec((N_HEADS_B, SEQ, LANES), lambda bi: (0, 0, 0)),
            pl.BlockSpec((MOBA_BLOCK, MOBA_BLOCK), lambda bi: (0, 0)),
        ],
        out_specs=group(0),
        out_shape=jax.ShapeDtypeStruct((batch, n_pairs, SEQ, LANES), BF16),
        scratch_shapes=[
            pltpu.VMEM((n_pairs, SEQ, 2 * LANES), BF16),
            pltpu.VMEM((n_pairs, SEQ, LANES), BF16),
            pltpu.VMEM((n_pairs, SEQ, LANES), BF16),
            pltpu.VMEM((n_pairs, 2 * V_ROWS, SEQ), BF16),
            pltpu.VMEM((n_pairs, 2, SEQ, MOBA_BLOCK), F32),
            pltpu.VMEM((n_pairs, 2, SEQ, MOBA_BLOCK), F32),
            pltpu.VMEM((n_pairs, 2, SEQ, MOBA_BLOCK), BF16),
            pltpu.VMEM((n_pairs, 2, SEQ, MOBA_BLOCK), BF16),
        ],
        compiler_params=pltpu.CompilerParams(
            dimension_semantics=("parallel",), vmem_limit_bytes=VMEM_LIMIT),
        name="moba_attention",
    )(qkv, qkv, qkv, kaug, causal)


def kernel(x, ffn1_gate, ffn1_up, ffn1_down, ln1_g, ln1_b, w_in, w_out, ln2_g, ln2_b,
           ffn2_gate, ffn2_up, ffn2_down, ln3_g, ln3_b):
    assert x.shape[1:] == (SEQ, D_MODEL) and ffn1_gate.shape == (DEPTH, D_MODEL, D_FF)
    bf = lambda w: w[0].astype(BF16)
    row = lambda p: p[0].reshape(1, D_MODEL).astype(F32)
    t12, t3 = _dilated_bias_tables()
    kaug, causal = _moba_tables()
    perm = _tile_permutation()

    h1, qkv = _ffn_qkv(x, bf(ffn1_gate), bf(ffn1_up), bf(ffn1_down), row(ln1_g), row(ln1_b), bf(w_in),
                       jnp.asarray(perm, BF16))
    oa = _dilated_attention(qkv, jnp.asarray(t12), jnp.asarray(t3))
    ob = _moba_attention(qkv, jnp.asarray(kaug), jnp.asarray(causal))
    w_o = bf(w_out)
    return _out_ffn(oa, ob, h1, jnp.asarray(perm.T, BF16), w_o[:WIDTH_A], w_o[WIDTH_A:], row(ln2_g), row(ln2_b),
                    bf(ffn2_gate), bf(ffn2_up), bf(ffn2_down), row(ln3_g), row(ln3_b))
```

```python
import numpy as np
import jax
import jax.numpy as jnp
from jax import lax
from jax.experimental import pallas as pl
from jax.experimental.pallas import tpu as pltpu

D_MODEL = 1024
SEQ = 2048
DEPTH = 1
HEAD_DIM = 64
N_HEADS = 16
N_HEADS_B = 4
N_HEADS_A = 12
WIDTH_A = N_HEADS_A * HEAD_DIM
WIDTH_B = N_HEADS_B * HEAD_DIM
MIX_WIDTH = WIDTH_A + WIDTH_B
DILATED_CONFIGS = ((128, 1), (512, 4), (2048, 16))
MOBA_BLOCK = 256
MOBA_TOPK = 3
D_FF = 2816
LN_EPS = 1e-5
ALPHA = (2.0 * DEPTH) ** 0.25

LANES = 128
QKV_GROUPS = 3 * MIX_WIDTH // LANES
SUBLANES = 8
BF16_ROWS = 16
N_CLASSES = 16
CLASS_ROWS = SEQ // N_CLASSES
BAND = 128
PAIRS_PER_STEP = 2
ROW_TILE = 512
OUT_ROW_TILE = 1024
TILE_CLASS_ROWS = ROW_TILE // N_CLASSES
SUB_TILE = 256
SUB_CLASS_ROWS = SUB_TILE // N_CLASSES
FF_CHUNK = 256
NEG = -1e30
LOG2E = 1.4426950408889634
VMEM_LIMIT = 56 * 1024 * 1024

F32 = jnp.float32
BF16 = jnp.bfloat16


def _dot(a, b):
    return jnp.dot(a, b, preferred_element_type=F32)


def _dot_nt(a, b, precision=None):
    return lax.dot_general(a, b, (((1,), (1,)), ((), ())), precision=precision,
                           preferred_element_type=F32)


def _layer_norm(z, g, b):
    mu = jnp.mean(z, axis=-1, keepdims=True)
    zc = z - mu
    var = jnp.mean(zc * zc, axis=-1, keepdims=True)
    return zc * lax.rsqrt(var + LN_EPS) * g + b


def _swiglu_steps(get_xb, wg_ref, wu_ref, act_ref):
    def chunk(c):
        cols = slice(c * FF_CHUNK, (c + 1) * FF_CHUNK)
        xb = get_xb()
        g = _dot(xb, wg_ref[:, cols])
        u = _dot(xb, wu_ref[:, cols])
        act_ref[:, cols] = (g / (1.0 + jnp.exp(-g)) * u).astype(BF16)

    return [lambda c=c: chunk(c) for c in range(D_FF // FF_CHUNK)]


def _emit_spread(main_steps, side_steps):
    gap = max(1, len(main_steps) // (len(side_steps) + 1))
    side = list(side_steps)
    for i, step in enumerate(main_steps):
        step()
        if side and (i + 1) % gap == 0:
            side.pop(0)()
    for step in side:
        step()


def _tile_permutation():
    u = np.arange(SUB_TILE)
    perm = np.zeros((SUB_TILE, SUB_TILE), np.float32)
    perm[u, N_CLASSES * (u % SUB_CLASS_ROWS) + u // SUB_CLASS_ROWS] = 1.0
    return perm


def _ffn_qkv_kernel(x_ref, wg_ref, wu_ref, wd_ref, g_ref, b_ref, win_ref, perm_ref,
                    h_ref, qkv_ref, *act_refs):
    def sub_tile(sub):
        rows = slice(sub * SUB_TILE, (sub + 1) * SUB_TILE)
        act_ref = act_refs[sub]
        val = {}

        def norm():
            x = x_ref[0, rows, :]
            y = _dot(act_ref[...], wd_ref[...])
            h = _layer_norm(ALPHA * x + 0.5 * y, g_ref[...], b_ref[...])
            h_ref[0, rows, :] = h
            val["hb"] = h.astype(BF16)

        def reorder():
            val["hb"] = _dot(perm_ref[...], val["hb"]).astype(BF16)

        def project():
            qkv = _dot(val.pop("hb"), win_ref[...])
            for c in range(QKV_GROUPS):
                for r in range(N_CLASSES):
                    qkv_ref[0, c, r, sub * SUB_CLASS_ROWS:(sub + 1) * SUB_CLASS_ROWS, :] = (
                        qkv[r * SUB_CLASS_ROWS:(r + 1) * SUB_CLASS_ROWS, c * LANES:(c + 1) * LANES])

        chunks = _swiglu_steps(lambda: x_ref[0, rows, :].astype(BF16), wg_ref, wu_ref, act_ref)
        return chunks, [norm, reorder, project]

    chunks0, (norm0, reorder0, project0) = sub_tile(0)
    chunks1, (norm1, reorder1, project1) = sub_tile(1)
    for step in chunks0:
        step()
    _emit_spread(chunks1, [norm0, reorder0])
    norm1()
    project0()
    reorder1()
    project1()


def _const_spec(shape):
    return pl.BlockSpec(shape, lambda *_: (0,) * len(shape), pipeline_mode=pl.Buffered(1))


def _ffn_qkv(x, wg, wu, wd, g, b, w_in, perm):
    batch = x.shape[0]
    n_tiles = SEQ // ROW_TILE
    h1, qkv = pl.pallas_call(
        _ffn_qkv_kernel,
        grid=(batch, n_tiles),
        in_specs=[
            pl.BlockSpec((1, ROW_TILE, D_MODEL), lambda bi, ti: (bi, ti, 0)),
            _const_spec((D_MODEL, D_FF)), _const_spec((D_MODEL, D_FF)), _const_spec((D_FF, D_MODEL)),
            _const_spec((1, D_MODEL)), _const_spec((1, D_MODEL)),
            _const_spec((D_MODEL, 3 * MIX_WIDTH)), _const_spec((SUB_TILE, SUB_TILE)),
        ],
        out_specs=[
            pl.BlockSpec((1, ROW_TILE, D_MODEL), lambda bi, ti: (bi, ti, 0)),
            pl.BlockSpec((1, QKV_GROUPS, N_CLASSES, TILE_CLASS_ROWS, LANES), lambda bi, ti: (bi, 0, 0, ti, 0)),
        ],
        out_shape=[
            jax.ShapeDtypeStruct((batch, SEQ, D_MODEL), F32),
            jax.ShapeDtypeStruct((batch, QKV_GROUPS, N_CLASSES, CLASS_ROWS, LANES), F32),
        ],
        scratch_shapes=[pltpu.VMEM((SUB_TILE, D_FF), BF16)] * (ROW_TILE // SUB_TILE),
        compiler_params=pltpu.CompilerParams(
            dimension_semantics=("parallel", "parallel"), vmem_limit_bytes=VMEM_LIMIT),
        name="ffn1_ln1_qkv",
    )(x, wg, wu, wd, g, b, w_in, perm)
    return h1, qkv.reshape(batch, QKV_GROUPS, SEQ, LANES)


def _out_ffn_kernel(oa_ref, ob_ref, h_ref, unperm_ref, woa_ref, wob_ref, g2_ref, b2_ref,
                    wg_ref, wu_ref, wd_ref, g3_ref, b3_ref, out_ref, *act_refs):
    def natural_rows(o_ref, sub):
        part = slice(sub * SUB_CLASS_ROWS, (sub + 1) * SUB_CLASS_ROWS)
        o_res = jnp.concatenate(
            [jnp.concatenate([o_ref[0, p, r, part, :] for p in range(o_ref.shape[1])], axis=1)
             for r in range(N_CLASSES)], axis=0)
        return _dot(unperm_ref[...], o_res).astype(BF16)

    def sub_tile(sub):
        rows = slice(sub * SUB_TILE, (sub + 1) * SUB_TILE)
        act_ref = act_refs[sub]
        val = {}

        def mix():
            val["mix"] = (_dot(natural_rows(oa_ref, sub), woa_ref[...])
                          + _dot(natural_rows(ob_ref, sub), wob_ref[...]))

        def norm():
            val["h2"] = _layer_norm(ALPHA * h_ref[0, rows, :] + val.pop("mix"), g2_ref[...], b2_ref[...])
            val["h2b"] = val["h2"].astype(BF16)

        def ffn_norm():
            y = _dot(act_ref[...], wd_ref[...])
            out_ref[0, rows, :] = _layer_norm(ALPHA * val.pop("h2") + 0.5 * y, g3_ref[...], b3_ref[...])

        chunks = _swiglu_steps(lambda: val["h2b"], wg_ref, wu_ref, act_ref)
        return mix, norm, chunks, ffn_norm

    subs = [sub_tile(sub) for sub in range(len(act_refs))]
    mix, norm, chunks, tail = zip(*subs)
    mix[0]()
    mix[1]()
    norm[0]()
    for sub in range(len(subs)):
        side = [tail[sub - 1]] if sub > 0 else []
        side += [norm[sub + 1]] if sub + 1 < len(subs) else []
        side += [mix[sub + 2]] if sub + 2 < len(subs) else []
        _emit_spread(chunks[sub], side)
    tail[-1]()


def _out_ffn(oa, ob, h1, unperm, woa, wob, g2, b2, wg, wu, wd, g3, b3):
    batch = h1.shape[0]
    n_tiles = SEQ // OUT_ROW_TILE
    res_spec = lambda width: pl.BlockSpec((1, width // LANES, N_CLASSES, OUT_ROW_TILE // N_CLASSES, LANES),
                                          lambda bi, ti: (bi, 0, 0, ti, 0))
    return pl.pallas_call(
        _out_ffn_kernel,
        grid=(batch, n_tiles),
        in_specs=[
            res_spec(WIDTH_A), res_spec(WIDTH_B),
            pl.BlockSpec((1, OUT_ROW_TILE, D_MODEL), lambda bi, ti: (bi, ti, 0)),
            _const_spec((SUB_TILE, SUB_TILE)),
            _const_spec((WIDTH_A, D_MODEL)), _const_spec((WIDTH_B, D_MODEL)),
            _const_spec((1, D_MODEL)), _const_spec((1, D_MODEL)),
            _const_spec((D_MODEL, D_FF)), _const_spec((D_MODEL, D_FF)), _const_spec((D_FF, D_MODEL)),
            _const_spec((1, D_MODEL)), _const_spec((1, D_MODEL)),
        ],
        out_specs=pl.BlockSpec((1, OUT_ROW_TILE, D_MODEL), lambda bi, ti: (bi, ti, 0)),
        out_shape=jax.ShapeDtypeStruct((batch, SEQ, D_MODEL), F32),
        scratch_shapes=[pltpu.VMEM((SUB_TILE, D_FF), BF16)] * (OUT_ROW_TILE // SUB_TILE),
        compiler_params=pltpu.CompilerParams(
            dimension_semantics=("parallel", "parallel"), vmem_limit_bytes=VMEM_LIMIT),
        name="outproj_ln2_ffn2_ln3",
    )(oa.reshape(batch, WIDTH_A // LANES, N_CLASSES, CLASS_ROWS, LANES),
      ob.reshape(batch, WIDTH_B // LANES, N_CLASSES, CLASS_ROWS, LANES),
      h1, unperm, woa, wob, g2, b2, wg, wu, wd, g3, b3)


def _alibi_slopes():
    idx = np.arange(N_HEADS)
    slopes = 2.0 ** (-8.0 * (idx + 1) / N_HEADS)
    is_b = (idx % 4) == 3
    return slopes[~is_b], slopes[is_b]


def _lane_is_head0(shape):
    return lax.broadcasted_iota(jnp.int32, shape, 1) < HEAD_DIM


def _gather_rows(ref, pieces):
    return jnp.concatenate([ref[0, s:s + n, :] for s, n in pieces], axis=0)


def _dilated_pieces(branch, blk):
    if branch == 0:
        return [(CLASS_ROWS * r + 8 * blk, 8) for r in range(N_CLASSES)]
    if branch == 1:
        r4, b = divmod(blk, 4)
        return [(CLASS_ROWS * (r4 + 4 * c) + 32 * b, 32) for c in range(4)]
    return [(CLASS_ROWS * blk, CLASS_ROWS)]


def _dilated_local_pos():
    u = np.arange(BAND)
    return [16 * (u % 8) + u // 8, 4 * (u % 32) + u // 32, u]


def _dilated_bias_tables():
    slopes_a, _ = _alibi_slopes()
    pos = _dilated_local_pos()
    t12 = np.zeros((N_HEADS_A, 2, BAND, 2 * BAND), np.float64)
    t3 = np.zeros((N_HEADS_A, BAND, BAND), np.float64)
    for br in range(2):
        dil = DILATED_CONFIGS[br][1]
        a_q = pos[br][:, None]
        j_band = np.concatenate([pos[br], BAND + pos[br]])[None, :]
        delta = a_q + BAND - j_band
        valid = (delta >= 0) & (delta <= BAND)
        for h in range(N_HEADS_A):
            t12[h, br] = np.where(valid, -LOG2E * slopes_a[h] * delta * dil, NEG)
    dil = DILATED_CONFIGS[2][1]
    delta = pos[2][:, None] - pos[2][None, :]
    for h in range(N_HEADS_A):
        t3[h] = np.where(delta >= 0, -LOG2E * slopes_a[h] * delta * dil, NEG)
    t12 = t12.reshape(N_HEADS_A // 2, 2, 2, BAND, 2 * BAND).transpose(0, 2, 1, 3, 4)
    return (t12.reshape(N_HEADS_A // 2, 2, 2 * BAND, 2 * BAND).astype(np.float32),
            t3.reshape(N_HEADS_A // 2, 2 * BAND, BAND).astype(np.float32))


def _dilated_kernel(q_ref, k_ref, v_ref, t12_ref, t3_ref, o_ref,
                    q2_ref, ks_ref, vs_ref, acc_ref, m_ref, l_ref):
    n_blocks = SEQ // BAND
    head0 = _lane_is_head0((BAND, LANES))
    ones = jnp.ones((2 * BAND, LANES), BF16)

    def per_head(x):
        return jnp.where(head0, x[:BAND], x[BAND:])

    def stage(pp, br, blk):
        gather = lambda ref: jnp.concatenate([ref[0, pp, s:s + n, :] for s, n in _dilated_pieces(br, blk)], axis=0)
        rows = slice(BAND * blk, BAND * (blk + 1))
        qblk = gather(q_ref) * (LOG2E * HEAD_DIM ** -0.5)
        q2_ref[pp, br, 2 * BAND * blk:2 * BAND * blk + BAND, :] = jnp.where(head0, qblk, 0.0).astype(BF16)
        q2_ref[pp, br, 2 * BAND * blk + BAND:2 * BAND * (blk + 1), :] = jnp.where(head0, 0.0, qblk).astype(BF16)
        ks_ref[pp, br, rows, :] = gather(k_ref).astype(BF16)
        vs_ref[pp, br, rows, :] = gather(v_ref).astype(BF16)

    def block_steps(pp, br, blk):
        first = (blk == 0) if br == 0 else (blk % 4 == 0) if br == 1 else True
        lo = BAND * blk if first else BAND * (blk - 1)
        keys = slice(lo, BAND * (blk + 1))
        n_keys = BAND * (blk + 1) - lo
        val = {}

        def scores():
            table = t3_ref[pp] if br == 2 else t12_ref[pp, br, :, 2 * BAND - n_keys:]
            s2 = _dot_nt(q2_ref[pp, br, 2 * BAND * blk:2 * BAND * (blk + 1), :], ks_ref[pp, br, keys, :]) + table
            val["s"], val["m"] = s2, jnp.max(s2, axis=1, keepdims=True)

        def weigh():
            p = jnp.exp2(val.pop("s") - val["m"]).astype(BF16)
            val["pv"] = _dot(p, jnp.concatenate([vs_ref[pp, br, keys, :], ones[:n_keys]], axis=1))

        def results():
            pv = val.pop("pv")
            acc, l = per_head(pv[:, :LANES]), per_head(pv[:, LANES:])
            m = per_head(jnp.broadcast_to(val.pop("m"), (2 * BAND, LANES)))
            if br < 2:
                off = 0
                for s, n in _dilated_pieces(br, blk):
                    acc_ref[pp, br, s:s + n, :] = acc[off:off + n]
                    m_ref[pp, br, s:s + n, :] = m[off:off + n]
                    l_ref[pp, br, s:s + n, :] = l[off:off + n]
                    off += n
                return
            rows = slice(CLASS_ROWS * blk, CLASS_ROWS * (blk + 1))
            m1, m2 = m_ref[pp, 0, rows, :], m_ref[pp, 1, rows, :]
            m_all = jnp.maximum(jnp.maximum(m1, m2), m)
            w1, w2, w3 = jnp.exp2(m1 - m_all), jnp.exp2(m2 - m_all), jnp.exp2(m - m_all)
            num = w1 * acc_ref[pp, 0, rows, :] + w2 * acc_ref[pp, 1, rows, :] + w3 * acc
            den = w1 * l_ref[pp, 0, rows, :] + w2 * l_ref[pp, 1, rows, :] + w3 * l
            o_ref[0, pp, rows, :] = (num / den).astype(BF16)

        return scores, weigh, results

    def pair_steps(pp):
        blocks = [block_steps(pp, br, blk) for br in range(3) for blk in range(n_blocks)]
        depth = 3

        def iteration(i):
            if i < 2 * n_blocks:
                stage(pp, 1 + i // n_blocks, i % n_blocks)
            for d in range(depth):
                if 0 <= i - d < len(blocks):
                    blocks[i - d][d]()

        first = [lambda blk=blk: stage(pp, 0, blk) for blk in range(n_blocks)]
        return first + [lambda i=i: iteration(i) for i in range(len(blocks) + depth - 1)]

    steps = [pair_steps(pp) for pp in range(PAIRS_PER_STEP)]
    offset = 1
    for i in range(len(steps[0]) + offset * (PAIRS_PER_STEP - 1)):
        for pp in range(PAIRS_PER_STEP):
            if 0 <= i - offset * pp < len(steps[pp]):
                steps[pp][i - offset * pp]()


def _dilated_attention(qkv, t12, t3):
    batch = qkv.shape[0]
    n_steps = N_HEADS_A // 2 // PAIRS_PER_STEP
    group = lambda base: pl.BlockSpec((1, PAIRS_PER_STEP, SEQ, LANES), lambda si, bi: (bi, base + si, 0, 0))
    return pl.pallas_call(
        _dilated_kernel,
        grid=(n_steps, batch),
        in_specs=[
            group(0), group(n_steps), group(2 * n_steps),
            pl.BlockSpec((PAIRS_PER_STEP, 2, 2 * BAND, 2 * BAND), lambda si, bi: (si, 0, 0, 0)),
            pl.BlockSpec((PAIRS_PER_STEP, 2 * BAND, BAND), lambda si, bi: (si, 0, 0)),
        ],
        out_specs=group(0),
        out_shape=jax.ShapeDtypeStruct((batch, N_HEADS_A // 2, SEQ, LANES), BF16),
        scratch_shapes=[
            pltpu.VMEM((PAIRS_PER_STEP, 3, 2 * SEQ, LANES), BF16),
            pltpu.VMEM((PAIRS_PER_STEP, 3, SEQ, LANES), BF16),
            pltpu.VMEM((PAIRS_PER_STEP, 3, SEQ, LANES), BF16),
            pltpu.VMEM((PAIRS_PER_STEP, 2, SEQ, LANES), F32),
            pltpu.VMEM((PAIRS_PER_STEP, 2, SEQ, LANES), F32),
            pltpu.VMEM((PAIRS_PER_STEP, 2, SEQ, LANES), F32),
        ],
        compiler_params=pltpu.CompilerParams(
            dimension_semantics=("parallel", "parallel"), vmem_limit_bytes=VMEM_LIMIT),
        name="dilated_attention",
    )(qkv, qkv, qkv, t12, t3)


N_MOBA_BLOCKS = SEQ // MOBA_BLOCK
MOBA_PIECE = MOBA_BLOCK // N_CLASSES
SEL_LANES = 2 * N_MOBA_BLOCKS
POS_PIECES = 3
V_ROWS = HEAD_DIM + BF16_ROWS


def _moba_pieces(blk):
    return [(CLASS_ROWS * r + MOBA_PIECE * blk, MOBA_PIECE) for r in range(N_CLASSES)]


def _moba_tables():
    _, slopes_b = _alibi_slopes()
    u = np.arange(MOBA_BLOCK)
    pos = N_CLASSES * (u % MOBA_PIECE) + u // MOBA_PIECE
    kaug = np.zeros((N_HEADS_B, SEQ, LANES), np.float32)
    for h in range(N_HEADS_B):
        for blk in range(N_MOBA_BLOCKS):
            rows = slice(MOBA_BLOCK * blk, MOBA_BLOCK * (blk + 1))
            kaug[h, rows, N_MOBA_BLOCKS * (h % 2) + blk] = 1.0
            rest = LOG2E * slopes_b[h] * (MOBA_BLOCK * blk + pos).astype(np.float64)
            for piece in range(POS_PIECES):
                part = rest.astype(BF16).astype(np.float64)
                kaug[h, rows, SEL_LANES + piece] = part
                rest = rest - part
    causal = np.where(pos[None, :] >= pos[:, None], 0.0, NEG).astype(np.float32)
    return kaug.astype(BF16), causal


def _moba_pair_steps(q_ref, k_ref, v_ref, kaug_ref, causal_ref, o_ref,
                     qs_ref, ka_ref, kb_ref, vt_ref, s_refs, p_refs, descending=False):
    head0 = _lane_is_head0((MOBA_BLOCK, LANES))
    n_sel = min(MOBA_TOPK, N_MOBA_BLOCKS - 1)
    ones_rows = jnp.ones((BF16_ROWS, MOBA_BLOCK), BF16)
    blk_id = lax.broadcasted_iota(jnp.int32, (N_MOBA_BLOCKS, MOBA_BLOCK), 0)
    pos_rows = jnp.where(blk_id < POS_PIECES, 1.0, 0.0)
    pad_rows = jnp.zeros((LANES - SEL_LANES - N_MOBA_BLOCKS, MOBA_BLOCK), F32)
    kmean_h = []

    def stage():
        kmean_rows = []
        for blk in range(N_MOBA_BLOCKS):
            pieces = _moba_pieces(blk)
            rows = slice(MOBA_BLOCK * blk, MOBA_BLOCK * (blk + 1))
            qs_ref[rows, :LANES] = (_gather_rows(q_ref, pieces) * (LOG2E * HEAD_DIM ** -0.5)).astype(BF16)
            kblk = _gather_rows(k_ref, pieces)
            ka_ref[rows, :] = jnp.where(head0, kblk, 0.0).astype(BF16)
            kb_ref[rows, :] = jnp.where(head0, 0.0, kblk).astype(BF16)
            vt = _gather_rows(v_ref, pieces).T
            for h in range(2):
                vt_ref[V_ROWS * h:V_ROWS * h + HEAD_DIM, rows] = vt[HEAD_DIM * h:HEAD_DIM * (h + 1)].astype(BF16)
                vt_ref[V_ROWS * h + HEAD_DIM:V_ROWS * (h + 1), rows] = ones_rows
            kmean_rows.append(jnp.sum(kblk, axis=0, keepdims=True) * (1.0 / MOBA_BLOCK))
        kmean = jnp.concatenate(kmean_rows, axis=0)
        head0_k = _lane_is_head0((N_MOBA_BLOCKS, LANES))
        kmean_h.extend([jnp.where(head0_k, kmean, 0.0), jnp.where(head0_k, 0.0, kmean)])

    def query_extra_lanes(n):
        sel = [jnp.zeros((N_MOBA_BLOCKS, MOBA_BLOCK), F32)] * 2
        if n > n_sel:
            q_f32 = _gather_rows(q_ref, _moba_pieces(n))
            for h in range(2):
                gate = _dot_nt(kmean_h[h], q_f32, precision=lax.Precision.HIGHEST)
                for m in range(n):
                    g_m = gate[m:m + 1, :]
                    beats = ((gate > g_m) | ((gate == g_m) & (blk_id < m))) & (blk_id < n)
                    rank = jnp.sum(beats.astype(F32), axis=0, keepdims=True)
                    sel[h] = jnp.where((blk_id == m) & (rank >= n_sel), NEG, sel[h])
        q_extra = jnp.concatenate([sel[0], sel[1], pos_rows, pad_rows], axis=0).T
        qs_ref[MOBA_BLOCK * n:MOBA_BLOCK * (n + 1), LANES:] = q_extra.astype(BF16)

    def score_steps(n, slot, fold):
        def step(m):
            k_rows = slice(MOBA_BLOCK * m, MOBA_BLOCK * (m + 1))
            k_aug = jnp.concatenate(
                [jnp.concatenate([ka_ref[k_rows, :], kaug_ref[0, k_rows, :]], axis=1),
                 jnp.concatenate([kb_ref[k_rows, :], kaug_ref[1, k_rows, :]], axis=1)], axis=0)
            s2 = _dot_nt(k_aug, qs_ref[MOBA_BLOCK * n:MOBA_BLOCK * (n + 1), :])
            for h in range(2):
                s = s2[h * MOBA_BLOCK:(h + 1) * MOBA_BLOCK]
                if m == n:
                    s = s + causal_ref[...]
                s_refs[slot][h, k_rows, :] = s
                parts = [s[SUBLANES * i:SUBLANES * (i + 1)] for i in range(MOBA_BLOCK // SUBLANES)]
                if fold[h] is not None:
                    parts.append(fold[h])
                while len(parts) > 1:
                    parts = [jnp.maximum(a, b) for a, b in zip(parts[::2], parts[1::2])] + parts[len(parts) & ~1:]
                fold[h] = parts[0]

        return [lambda m=m: step(m) for m in range(n + 1)]

    def output_steps(n, slot, fold):
        n_keys = MOBA_BLOCK * (n + 1)
        mx = []

        def probs(m):
            if not mx:
                mx.extend(jnp.max(fold[h], axis=0, keepdims=True) for h in range(2))
            k_rows = slice(MOBA_BLOCK * m, MOBA_BLOCK * (m + 1))
            for h in range(2):
                p_refs[slot][h, k_rows, :] = jnp.exp2(s_refs[slot][h, k_rows, :] - mx[h]).astype(BF16)

        def finish():
            acc = [_dot(vt_ref[V_ROWS * h:V_ROWS * (h + 1), :n_keys], p_refs[slot][h, :n_keys, :])
                   for h in range(2)]
            out_t = jnp.concatenate(
                [acc[h][:HEAD_DIM] / acc[h][HEAD_DIM:HEAD_DIM + 1] for h in range(2)], axis=0)
            out = out_t.T.astype(BF16)
            off = 0
            for s, cnt in _moba_pieces(n):
                o_ref[0, s:s + cnt, :] = out[off:off + cnt]
                off += cnt

        return [lambda m=m: probs(m) for m in range(n + 1)] + [finish]

    order = list(range(N_MOBA_BLOCKS))
    late = [n for n in order if n > n_sel]
    emit = [stage] + [lambda n=n: query_extra_lanes(n) for n in order if n <= n_sel]
    if descending:
        order, emit, late = order[::-1], emit + [lambda n=n: query_extra_lanes(n) for n in late], []
    folds = [[None, None] for _ in range(N_MOBA_BLOCKS)]
    for pos in range(N_MOBA_BLOCKS + 1):
        first = score_steps(order[pos], pos % 2, folds[order[pos]]) if pos < N_MOBA_BLOCKS else []
        second = output_steps(order[pos - 1], (pos - 1) % 2, folds[order[pos - 1]]) if pos > 0 else []
        third = [lambda n=n: query_extra_lanes(n) for n in late[2 * pos:2 * pos + 2]]
        for i in range(max(len(first), len(second), len(third))):
            emit.extend(steps[i] for steps in (first, second, third) if i < len(steps))
    return emit


def _moba_kernel(q_ref, k_ref, v_ref, kaug_ref, causal_ref, o_ref,
                 qs_ref, ka_ref, kb_ref, vt_ref, s0_ref, s1_ref, p0_ref, p1_ref):
    steps = []
    for pp in range(N_HEADS_B // 2):
        steps.append(_moba_pair_steps(
            q_ref.at[:, pp], k_ref.at[:, pp], v_ref.at[:, pp],
            kaug_ref.at[2 * pp:2 * pp + 2], causal_ref, o_ref.at[:, pp],
            qs_ref.at[pp], ka_ref.at[pp], kb_ref.at[pp], vt_ref.at[pp],
            (s0_ref.at[pp], s1_ref.at[pp]), (p0_ref.at[pp], p1_ref.at[pp]), descending=pp == 1))
    offset = 1
    for i in range(len(steps[0]) + offset):
        if i < len(steps[0]):
            steps[0][i]()
        if 0 <= i - offset < len(steps[1]):
            steps[1][i - offset]()


def _moba_attention(qkv, kaug, causal):
    batch = qkv.shape[0]
    n_pairs = N_HEADS_B // 2
    group = lambda idx: pl.BlockSpec((1, n_pairs, SEQ, LANES), lambda bi: (bi, idx, 0, 0))
    base_q = 3 * WIDTH_A // WIDTH_B
    return pl.pallas_call(
        _moba_kernel,
        grid=(batch,),
        in_specs=[
            group(base_q), group(base_q + 1), group(base_q + 2),
            pl.BlockSpec((N_HEADS_B, SEQ, LANES), lambda bi: (0, 0, 0)),
            pl.BlockSpec((MOBA_BLOCK, MOBA_BLOCK), lambda bi: (0, 0)),
        ],
        out_specs=group(0),
        out_shape=jax.ShapeDtypeStruct((batch, n_pairs, SEQ, LANES), BF16),
        scratch_shapes=[
            pltpu.VMEM((n_pairs, SEQ, 2 * LANES), BF16),
            pltpu.VMEM((n_pairs, SEQ, LANES), BF16),
            pltpu.VMEM((n_pairs, SEQ, LANES), BF16),
            pltpu.VMEM((n_pairs, 2 * V_ROWS, SEQ), BF16),
            pltpu.VMEM((n_pairs, 2, SEQ, MOBA_BLOCK), F32),
            pltpu.VMEM((n_pairs, 2, SEQ, MOBA_BLOCK), F32),
            pltpu.VMEM((n_pairs, 2, SEQ, MOBA_BLOCK), BF16),
            pltpu.VMEM((n_pairs, 2, SEQ, MOBA_BLOCK), BF16),
        ],
        compiler_params=pltpu.CompilerParams(
            dimension_semantics=("parallel",), vmem_limit_bytes=VMEM_LIMIT),
        name="moba_attention",
    )(qkv, qkv, qkv, kaug, causal)


def kernel(x, ffn1_gate, ffn1_up, ffn1_down, ln1_g, ln1_b, w_in, w_out, ln2_g, ln2_b,
           ffn2_gate, ffn2_up, ffn2_down, ln3_g, ln3_b):
    assert x.shape[1:] == (SEQ, D_MODEL) and ffn1_gate.shape == (DEPTH, D_MODEL, D_FF)
    bf = lambda w: w[0].astype(BF16)
    row = lambda p: p[0].reshape(1, D_MODEL).astype(F32)
    t12, t3 = _dilated_bias_tables()
    kaug, causal = _moba_tables()
    perm = _tile_permutation()

    h1, qkv = _ffn_qkv(x, bf(ffn1_gate), bf(ffn1_up), bf(ffn1_down), row(ln1_g), row(ln1_b), bf(w_in),
                       jnp.asarray(perm, BF16))
    oa = _dilated_attention(qkv, jnp.asarray(t12), jnp.asarray(t3))
    ob = _moba_attention(qkv, jnp.asarray(kaug), jnp.asarray(causal))
    w_o = bf(w_out)
    return _out_ffn(oa, ob, h1, jnp.asarray(perm.T, BF16), w_o[:WIDTH_A], w_o[WIDTH_A:], row(ln2_g), row(ln2_b),
                    bf(ffn2_gate), bf(ffn2_up), bf(ffn2_down), row(ln3_g), row(ln3_b))
```
